```python
import jax
import jax.numpy as jnp
from jax import lax
import numpy as np

D_MODEL = 4096
BATCH = 4
SEQ = 4096
DEPTH = 1

HEAD_DIM = 128
ATTN_HEADS = D_MODEL // HEAD_DIM
ATTN_KV_HEADS = ATTN_HEADS // 4
Q_WIDTH = ATTN_HEADS * HEAD_DIM
KV_WIDTH = ATTN_KV_HEADS * HEAD_DIM
WINDOW = 128
BLOCK = 128
ATTN_SCALE = HEAD_DIM ** -0.5
ROPE_DIM = HEAD_DIM // 4
ROPE_THETA = 500000.0
LRU_WIDTH = D_MODEL
LRU_BLOCK_DIM = 256
LRU_BLOCKS = LRU_WIDTH // LRU_BLOCK_DIM
CONV_WIDTH = 4
LRU_C = 8.0
MEM_LEN = 256
MEM_HEADS = 4
MEM_HEAD_DIM = 128
MEM_WIDTH = MEM_HEADS * MEM_HEAD_DIM
MEM_SCALE = MEM_HEAD_DIM ** -0.5
N_EXPERTS = 32
TOP_K = 4
EXPERT_DIM = (3 * D_MODEL) // 8
SWIGLU_LIMIT = 7.0
SWIGLU_ALPHA = 1.702
LN_EPS = 1e-5
DEEPNORM_ALPHA = (2.0 * DEPTH) ** 0.25
DEEPNORM_BETA = (8.0 * DEPTH) ** -0.25
IN_SPLITS = (Q_WIDTH, KV_WIDTH, KV_WIDTH, LRU_WIDTH, LRU_WIDTH, D_MODEL, D_MODEL)
IN_WIDTH = sum(IN_SPLITS)

kernel_name = 'hybrid_swa_rglru_memxattn_moe_layer'


def split_columns(h, sizes):
    idx = []
    acc = 0
    for s in sizes[:-1]:
        acc += s
        idx.append(acc)
    return jnp.split(h, idx, axis=-1)


def layer_norm(x, g, b):
    xf = x.astype(jnp.float32)
    mu = jnp.mean(xf, axis=-1, keepdims=True)
    var = jnp.mean(jnp.square(xf - mu), axis=-1, keepdims=True)
    y = (xf - mu) * lax.rsqrt(var + LN_EPS) * g.astype(jnp.float32) + b.astype(jnp.float32)
    return y.astype(x.dtype)


def partial_rope(t, positions):
    inv_freq = 1.0 / (ROPE_THETA ** (jnp.arange(0, ROPE_DIM, 2, dtype=jnp.float32) / ROPE_DIM))
    ang = positions.astype(jnp.float32)[..., None] * inv_freq
    cos = jnp.cos(ang)[:, :, None, :]
    sin = jnp.sin(ang)[:, :, None, :]
    tr = t[..., :ROPE_DIM].astype(jnp.float32)
    t1, t2 = tr[..., :ROPE_DIM // 2], tr[..., ROPE_DIM // 2:]
    rot = jnp.concatenate([t1 * cos - t2 * sin, t2 * cos + t1 * sin], axis=-1)
    return jnp.concatenate([rot.astype(t.dtype), t[..., ROPE_DIM:]], axis=-1)


def sliding_window_attention(q, k, v, sinks):
    B, S, Hq, Dh = q.shape
    Hkv = k.shape[2]
    G = Hq // Hkv
    nb = S // BLOCK
    qb = q.reshape(B, nb, BLOCK, Hkv, G, Dh)

    def with_prev(t):
        tb = t.reshape(B, nb, BLOCK, Hkv, Dh)
        prev = jnp.pad(tb[:, :-1], ((0, 0), (1, 0), (0, 0), (0, 0), (0, 0)))
        return jnp.concatenate([prev, tb], axis=2)

    kb = with_prev(k)
    vb = with_prev(v)
    scores = jnp.einsum('bnqhgd,bnshd->bnhgqs', qb, kb,
                        preferred_element_type=jnp.float32) * ATTN_SCALE
    qi = jnp.arange(BLOCK)[:, None]
    si = jnp.arange(2 * BLOCK)[None, :]
    delta = BLOCK + qi - si
    band = (delta >= 0) & (delta < WINDOW)
    blk = jnp.arange(nb)[:, None, None]
    mask = band[None] & ((blk > 0) | (si >= BLOCK)[None])
    scores = jnp.where(mask[None, :, None, None], scores, -jnp.inf)
    sink = sinks.astype(jnp.float32).reshape(Hkv, G)[None, None, :, :, None, None]
    m = jnp.maximum(jnp.max(scores, axis=-1, keepdims=True), sink)
    p = jnp.exp(scores - m)
    probs = p / (jnp.sum(p, axis=-1, keepdims=True) + jnp.exp(sink - m))
    out = jnp.einsum('bnhgqs,bnshd->bnqhgd', probs.astype(v.dtype), vb)
    return out.reshape(B, S, Hq * Dh)


def rglru_branch(xl, yl, conv_w, conv_b, w_ra, b_ra, w_ri, b_ri, lam):
    B, S, C = xl.shape
    xc = lax.conv_general_dilated(xl, conv_w[:, None, :], window_strides=(1,),
                                  padding=[(CONV_WIDTH - 1, 0)],
                                  dimension_numbers=('NWC', 'WIO', 'NWC'),
                                  feature_group_count=C) + conv_b
    xb = xc.reshape(B, S, LRU_BLOCKS, LRU_BLOCK_DIM)
    r = jax.nn.sigmoid(jnp.einsum('bshi,hij->bshj', xb, w_ra) + b_ra).reshape(B, S, C)
    i = jax.nn.sigmoid(jnp.einsum('bshi,hij->bshj', xb, w_ri) + b_ri).reshape(B, S, C)
    log_a = -LRU_C * r.astype(jnp.float32) * jax.nn.softplus(-lam.astype(jnp.float32))
    a = jnp.exp(log_a)
    gated_x = jnp.sqrt(-jnp.expm1(2.0 * log_a)) * (i * xc).astype(jnp.float32)

    def combine(left, right):
        a1, b1 = left
        a2, b2 = right
        return a1 * a2, a2 * b1 + b2

    _, h = lax.associative_scan(combine, (a, gated_x), axis=1)
    return h.astype(xl.dtype) * jax.nn.gelu(yl)


def memory_cross_attention(x, mem, w_q, w_kv, w_o):
    B, S, _ = x.shape
    M = mem.shape[1]
    q = (x @ w_q).reshape(B, S, MEM_HEADS, MEM_HEAD_DIM)
    k, v = jnp.split(mem @ w_kv, 2, axis=-1)
    k = k.reshape(B, M, MEM_HEADS, MEM_HEAD_DIM)
    v = v.reshape(B, M, MEM_HEADS, MEM_HEAD_DIM)
    s = jnp.einsum('bshd,bmhd->bhsm', q, k, preferred_element_type=jnp.float32) * MEM_SCALE
    p = jax.nn.softmax(s, axis=-1).astype(v.dtype)
    o = jnp.einsum('bhsm,bmhd->bshd', p, v).reshape(B, S, MEM_WIDTH)
    return o @ w_o


def moe_ffn(x, w_router, b_router, w_gate_up, b_gate_up, w_down, b_down):
    B, S, D = x.shape
    xf = x.reshape(B * S, D)
    logits = (xf @ w_router).astype(jnp.float32) + b_router.astype(jnp.float32)
    top_vals, top_idx = lax.top_k(logits, TOP_K)
    top_w = jax.nn.softmax(top_vals, axis=-1)
    combine_w = jnp.sum(jax.nn.one_hot(top_idx, N_EXPERTS, dtype=jnp.float32)
                        * top_w[..., None], axis=1)

    def expert_step(acc, ew):
        wgu, bgu, wd, bd, cw = ew
        h = xf @ wgu + bgu
        gate = jnp.minimum(h[:, :EXPERT_DIM], SWIGLU_LIMIT)
        up = jnp.clip(h[:, EXPERT_DIM:], -SWIGLU_LIMIT, SWIGLU_LIMIT)
        glu = gate * jax.nn.sigmoid(SWIGLU_ALPHA * gate)
        out = ((up + 1.0) * glu) @ wd + bd
        return acc + cw[:, None] * out.astype(jnp.float32), None

    acc0 = jnp.zeros((B * S, D), jnp.float32)
    acc, _ = lax.scan(expert_step, acc0, (w_gate_up, b_gate_up, w_down, b_down, combine_w.T))
    return acc.astype(x.dtype).reshape(B, S, D)


def setup_inputs(seed: int = 0) -> dict:
    key = jax.random.key(seed)
    k = jax.random.split(key, 32)

    def nrm(kk, shape, scale):
        return jax.random.normal(kk, shape, jnp.float32) * scale

    L = DEPTH
    a0 = jax.random.uniform(k[9], (L, LRU_WIDTH), jnp.float32, minval=0.9, maxval=0.999)
    offsets = jax.random.randint(k[2], (BATCH, 1), 0, 1024, dtype=jnp.int32)
    return {
        'x': nrm(k[0], (BATCH, SEQ, D_MODEL), 1.0),
        'mem': nrm(k[1], (BATCH, MEM_LEN, D_MODEL), 1.0),
        'positions': offsets + jnp.arange(SEQ, dtype=jnp.int32)[None, :],
        'w_in': nrm(k[3], (L, D_MODEL, IN_WIDTH), D_MODEL ** -0.5),
        'b_gate': nrm(k[4], (L, 2, D_MODEL), 0.02),
        'attn_sinks': nrm(k[5], (L, ATTN_HEADS), 0.5),
        'conv_w': nrm(k[6], (L, CONV_WIDTH, LRU_WIDTH), CONV_WIDTH ** -0.5),
        'conv_b': nrm(k[7], (L, LRU_WIDTH), 0.02),
        'w_lru_a': nrm(k[8], (L, LRU_BLOCKS, LRU_BLOCK_DIM, LRU_BLOCK_DIM), LRU_BLOCK_DIM ** -0.5),
        'b_lru_a': nrm(k[10], (L, LRU_BLOCKS, LRU_BLOCK_DIM), 0.02),
        'w_lru_i': nrm(k[11], (L, LRU_BLOCKS, LRU_BLOCK_DIM, LRU_BLOCK_DIM), LRU_BLOCK_DIM ** -0.5),
        'b_lru_i': nrm(k[12], (L, LRU_BLOCKS, LRU_BLOCK_DIM), 0.02),
        'lru_lambda': jnp.log(a0) - jnp.log1p(-a0),
        'w_branch_attn': nrm(k[13], (L, Q_WIDTH, D_MODEL), DEEPNORM_BETA * Q_WIDTH ** -0.5),
        'w_branch_lru': nrm(k[14], (L, LRU_WIDTH, D_MODEL), DEEPNORM_BETA * LRU_WIDTH ** -0.5),
        'w_mix_out': nrm(k[15], (L, D_MODEL, D_MODEL), DEEPNORM_BETA * D_MODEL ** -0.5),
        'ln1_g': 1.0 + nrm(k[16], (L, D_MODEL), 0.02),
        'ln1_b': nrm(k[17], (L, D_MODEL), 0.02),
        'w_mem_q': nrm(k[18], (L, D_MODEL, MEM_WIDTH), D_MODEL ** -0.5),
        'w_mem_kv': nrm(k[19], (L, D_MODEL, 2 * MEM_WIDTH), D_MODEL ** -0.5),
        'w_mem_o': nrm(k[20], (L, MEM_WIDTH, D_MODEL), DEEPNORM_BETA * MEM_WIDTH ** -0.5),
        'ln2_g': 1.0 + nrm(k[21], (L, D_MODEL), 0.02),
        'ln2_b': nrm(k[22], (L, D_MODEL), 0.02),
        'w_router': nrm(k[23], (L, D_MODEL, N_EXPERTS), D_MODEL ** -0.5),
        'b_router': nrm(k[24], (L, N_EXPERTS), 0.01),
        'w_gate_up': nrm(k[25], (L, N_EXPERTS, D_MODEL, 2 * EXPERT_DIM), D_MODEL ** -0.5),
        'b_gate_up': nrm(k[26], (L, N_EXPERTS, 2 * EXPERT_DIM), 0.02),
        'w_down': nrm(k[27], (L, N_EXPERTS, EXPERT_DIM, D_MODEL), DEEPNORM_BETA * EXPERT_DIM ** -0.5),
        'b_down': nrm(k[28], (L, N_EXPERTS, D_MODEL), 0.02),
        'ln3_g': 1.0 + nrm(k[29], (L, D_MODEL), 0.02),
        'ln3_b': nrm(k[30], (L, D_MODEL), 0.02),
    }


def reference(x, mem, positions, w_in, b_gate, attn_sinks, conv_w, conv_b, w_lru_a, b_lru_a,
              w_lru_i, b_lru_i, lru_lambda, w_branch_attn, w_branch_lru, w_mix_out, ln1_g, ln1_b,
              w_mem_q, w_mem_kv, w_mem_o, ln2_g, ln2_b, w_router, b_router, w_gate_up, b_gate_up,
              w_down, b_down, ln3_g, ln3_b):
    B, S, _ = x.shape
    for l in range(DEPTH):
        h = x @ w_in[l]
        q, k, v, xl, yl, ga, gl = split_columns(h, IN_SPLITS)
        q = partial_rope(q.reshape(B, S, ATTN_HEADS, HEAD_DIM), positions)
        k = partial_rope(k.reshape(B, S, ATTN_KV_HEADS, HEAD_DIM), positions)
        v = v.reshape(B, S, ATTN_KV_HEADS, HEAD_DIM)
        o_attn = sliding_window_attention(q, k, v, attn_sinks[l])
        o_lru = rglru_branch(xl, yl, conv_w[l], conv_b[l], w_lru_a[l], b_lru_a[l],
                             w_lru_i[l], b_lru_i[l], lru_lambda[l])
        g_attn = jax.nn.sigmoid(ga + b_gate[l, 0])
        g_lru = jax.nn.sigmoid(gl + b_gate[l, 1])
        mixed = g_attn * (o_attn @ w_branch_attn[l]) + g_lru * (o_lru @ w_branch_lru[l])
        x = layer_norm(DEEPNORM_ALPHA * x + mixed @ w_mix_out[l], ln1_g[l], ln1_b[l])
        xa = memory_cross_attention(x, mem, w_mem_q[l], w_mem_kv[l], w_mem_o[l])
        x = layer_norm(DEEPNORM_ALPHA * x + xa, ln2_g[l], ln2_b[l])
        ff = moe_ffn(x, w_router[l], b_router[l], w_gate_up[l], b_gate_up[l], w_down[l], b_down[l])
        x = layer_norm(DEEPNORM_ALPHA * x + ff, ln3_g[l], ln3_b[l])
    return x
```

```python
import functools

import numpy as np
import jax
import jax.numpy as jnp
from jax import lax
from jax.experimental import pallas as pl
from jax.experimental.pallas import tpu as pltpu

F32 = jnp.float32
BF16 = jnp.bfloat16

HEAD_DIM = 128
KV_GROUP = 4
ATTN_BLOCK = 128
ROPE_DIM = HEAD_DIM // 4
ROPE_THETA = 500000.0
LRU_BLOCK_DIM = 256
CONV_WIDTH = 4
LRU_C = 8.0
MEM_HEADS = 4
MEM_HEAD_DIM = 128
N_EXPERTS = 32
TOP_K = 4
SWIGLU_LIMIT = 7.0
SWIGLU_ALPHA = 1.702
LN_EPS = 1e-5
DEPTH = 1
DEEPNORM_ALPHA = (2.0 * DEPTH) ** 0.25

V7X_LANES = 128
V7X_SUBLANES = 8
V7X_VMEM_LIMIT_BYTES = 56 * 1024 * 1024

MOE_ROW_TILE = 512
MOE_F_TILE = 512
MOE_N_TILE = 1024
LRU_TIME_TILE = 512
DISPATCH_TILE = 128


def _cparams(n_axes):
    return pltpu.CompilerParams(dimension_semantics=("arbitrary",) * n_axes,
                                vmem_limit_bytes=V7X_VMEM_LIMIT_BYTES)


def _layer_norm_rows(y, g, b):
    mu = jnp.mean(y, axis=-1, keepdims=True)
    d = y - mu
    var = jnp.mean(d * d, axis=-1, keepdims=True)
    return d * lax.rsqrt(var + LN_EPS) * g + b


def _pack_bf16_pairs(x):
    n = x.shape[1] // 2
    u = pltpu.bitcast(x, jnp.uint32)
    r = (u + jnp.uint32(0x7FFF) + ((u >> 16) & jnp.uint32(1))) >> 16
    return (r[:, n:] << 16) | r[:, :n]


def _unpack_bf16_pairs(w):
    lo = pltpu.bitcast(w << 16, F32).astype(BF16)
    hi = pltpu.bitcast(w & jnp.uint32(0xFFFF0000), F32).astype(BF16)
    return lo, hi


def _mm_kernel(a_ref, b_ref, o_ref):
    o_ref[...] = jnp.dot(a_ref[...].astype(BF16), b_ref[...].astype(BF16),
                         preferred_element_type=F32).astype(o_ref.dtype)


def _mm_res_kernel(a_ref, b_ref, r_ref, o_ref, *, alpha):
    acc = jnp.dot(a_ref[...].astype(BF16), b_ref[...].astype(BF16), preferred_element_type=F32)
    o_ref[...] = (alpha * r_ref[...] + acc).astype(o_ref.dtype)


def matmul(a, b, *, tm, tn, out_dtype, res=None, alpha=1.0, name="matmul"):
    M, K = a.shape
    N = b.shape[1]
    assert M % tm == 0 and N % tn == 0
    in_specs = [pl.BlockSpec((tm, K), lambda i, j: (i, 0)),
                pl.BlockSpec((K, tn), lambda i, j: (0, j))]
    args = [a, b]
    if res is None:
        body = _mm_kernel
    else:
        body = functools.partial(_mm_res_kernel, alpha=alpha)
        in_specs.append(pl.BlockSpec((tm, tn), lambda i, j: (i, j)))
        args.append(res)
    return pl.pallas_call(
        body,
        grid=(M // tm, N // tn),
        in_specs=in_specs,
        out_specs=pl.BlockSpec((tm, tn), lambda i, j: (i, j)),
        out_shape=jax.ShapeDtypeStruct((M, N), out_dtype),
        compiler_params=_cparams(2),
        name=name,
    )(*args)


def _rope_kernel(pos_ref, freq_ref, h_ref, o_ref, *, n_q_heads, scale):
    ang = pos_ref[...] * freq_ref[...]
    lane = lax.broadcasted_iota(jnp.int32, ang.shape, 1)
    cos = jnp.cos(ang)
    sin = jnp.sin(ang)
    half = ROPE_DIM // 2
    c_mul = jnp.where(lane < ROPE_DIM, cos, 1.0)
    s_lo = jnp.where(lane < half, -sin, 0.0)
    s_hi = jnp.where((lane >= half) & (lane < ROPE_DIM), sin, 0.0)
    n_heads = h_ref.shape[1] // HEAD_DIM
    for hd in range(n_heads):
        t = h_ref[:, hd * HEAD_DIM:(hd + 1) * HEAD_DIM].astype(F32)
        up = pltpu.roll(t, HEAD_DIM - half, axis=1)
        dn = pltpu.roll(t, half, axis=1)
        r = t * c_mul + up * s_lo + dn * s_hi
        if hd < n_q_heads:
            r = r * scale
        o_ref[:, hd * HEAD_DIM:(hd + 1) * HEAD_DIM] = r.astype(o_ref.dtype)


def rope(h, pos_col, freq_row, *, n_q_heads, n_k_heads, scale, tm):
    T = h.shape[0]
    width = (n_q_heads + n_k_heads) * HEAD_DIM
    return pl.pallas_call(
        functools.partial(_rope_kernel, n_q_heads=n_q_heads, scale=scale),
        grid=(T // tm,),
        in_specs=[pl.BlockSpec((tm, 1), lambda i: (i, 0)),
                  pl.BlockSpec((1, HEAD_DIM), lambda i: (0, 0)),
                  pl.BlockSpec((tm, width), lambda i: (i, 0))],
        out_specs=pl.BlockSpec((tm, width), lambda i: (i, 0)),
        out_shape=jax.ShapeDtypeStruct((T, width), BF16),
        compiler_params=_cparams(1),
        name="rope",
    )(pos_col, freq_row, h)


def _swa_kernel(bias_ref, sink_ref, q_ref, kc_ref, kp_ref, vc_ref, vp_ref, o_ref):
    q = q_ref[...]
    q4 = jnp.concatenate([q[:, i * HEAD_DIM:(i + 1) * HEAD_DIM] for i in range(KV_GROUP)], axis=0)
    k = jnp.concatenate([kp_ref[...], kc_ref[...]], axis=0)
    v = jnp.concatenate([vp_ref[...], vc_ref[...]], axis=0)
    s = lax.dot_general(q4, k, (((1,), (1,)), ((), ())), preferred_element_type=F32)
    s = s + bias_ref[0]
    sink = sink_ref[0][:, :1]
    m = jnp.maximum(jnp.max(s, axis=-1, keepdims=True), sink)
    p = jnp.exp(s - m)
    denom = jnp.sum(p, axis=-1, keepdims=True) + jnp.exp(sink - m)
    o = jnp.dot(p.astype(BF16), v, preferred_element_type=F32) / denom
    o_ref[...] = jnp.concatenate(
        [o[i * ATTN_BLOCK:(i + 1) * ATTN_BLOCK] for i in range(KV_GROUP)], axis=1).astype(o_ref.dtype)


def sliding_window_attention(qk, h, bias, sink_rows, *, batch, seq, n_kv_heads, k_col0, v_col0):
    nb = seq // ATTN_BLOCK
    T = batch * seq
    qw = KV_GROUP * HEAD_DIM

    def cur(b, n, g):
        return b * nb + n

    def prev(b, n, g):
        return b * nb + jnp.maximum(n - 1, 0)

    return pl.pallas_call(
        _swa_kernel,
        grid=(batch, nb, n_kv_heads),
        in_specs=[
            pl.BlockSpec((1, KV_GROUP * ATTN_BLOCK, 2 * ATTN_BLOCK), lambda b, n, g: (jnp.minimum(n, 1), 0, 0)),
            pl.BlockSpec((1, KV_GROUP * ATTN_BLOCK, V7X_LANES), lambda b, n, g: (g, 0, 0)),
            pl.BlockSpec((ATTN_BLOCK, qw), lambda b, n, g: (cur(b, n, g), g)),
            pl.BlockSpec((ATTN_BLOCK, HEAD_DIM), lambda b, n, g: (cur(b, n, g), k_col0 + g)),
            pl.BlockSpec((ATTN_BLOCK, HEAD_DIM), lambda b, n, g: (prev(b, n, g), k_col0 + g)),
            pl.BlockSpec((ATTN_BLOCK, HEAD_DIM), lambda b, n, g: (cur(b, n, g), v_col0 + g)),
            pl.BlockSpec((ATTN_BLOCK, HEAD_DIM), lambda b, n, g: (prev(b, n, g), v_col0 + g)),
        ],
        out_specs=pl.BlockSpec((ATTN_BLOCK, qw), lambda b, n, g: (cur(b, n, g), g)),
        out_shape=jax.ShapeDtypeStruct((T, n_kv_heads * qw), BF16),
        compiler_params=_cparams(3),
        name="swa",
    )(bias, sink_rows, qk, qk, qk, h, h)


def _swa_bias():
    qi = np.arange(ATTN_BLOCK)[:, None]
    si = np.arange(2 * ATTN_BLOCK)[None, :]
    delta = ATTN_BLOCK + qi - si
    band = (delta >= 0) & (delta < ATTN_BLOCK)
    first = band & (si >= ATTN_BLOCK)
    m = np.stack([first, band]).astype(np.float32)
    bias = np.where(m > 0, 0.0, -np.inf).astype(np.float32)
    return np.tile(bias, (1, KV_GROUP, 1))


def _gelu_tanh(y):
    return 0.5 * y * (1.0 + jnp.tanh(0.7978845608028654 * (y + 0.044715 * (y * y * y))))


def _lru_kernel(x_ref, y_ref, cw_ref, cb_ref, wa_ref, ba_ref, wi_ref, bi_ref, lam_ref, o_ref,
                tail_ref, carry_ref):
    tt = x_ref.shape[0]

    @pl.when(pl.program_id(2) == 0)
    def _():
        tail_ref[...] = jnp.zeros_like(tail_ref)
        carry_ref[...] = jnp.zeros_like(carry_ref)

    x = x_ref[...].astype(F32)
    ext = jnp.concatenate([tail_ref[...], x], axis=0)
    tail_ref[...] = x[tt - V7X_SUBLANES:, :]
    cw = cw_ref[...]
    xc = cb_ref[...] + cw[CONV_WIDTH - 1:CONV_WIDTH, :] * x
    for j in range(1, CONV_WIDTH):
        sh = pltpu.roll(ext, j, axis=0)[V7X_SUBLANES:, :]
        xc = xc + cw[CONV_WIDTH - 1 - j:CONV_WIDTH - j, :] * sh
    xcb = xc.astype(BF16)
    r = jax.nn.sigmoid(jnp.dot(xcb, wa_ref[0].astype(BF16), preferred_element_type=F32) + ba_ref[0])
    gi = jax.nn.sigmoid(jnp.dot(xcb, wi_ref[0].astype(BF16), preferred_element_type=F32) + bi_ref[0])
    z = -lam_ref[...]
    softplus = jnp.maximum(z, 0.0) + jnp.log(1.0 + jnp.exp(-jnp.abs(z)))
    log_a = (-LRU_C) * r * softplus
    a = jnp.exp(log_a)
    b = jnp.sqrt(1.0 - a * a) * (gi * xc)

    row = lax.broadcasted_iota(jnp.int32, a.shape, 0)
    d = 1
    while d < tt:
        if d < V7X_SUBLANES:
            a_sh = jnp.where(row >= d, pltpu.roll(a, d, axis=0), 1.0)
            b_sh = jnp.where(row >= d, pltpu.roll(b, d, axis=0), 0.0)
            b = a * b_sh + b
            a = a * a_sh
        else:
            b = jnp.concatenate([b[:d], a[d:] * b[:-d] + b[d:]], axis=0)
            a = jnp.concatenate([a[:d], a[d:] * a[:-d]], axis=0)
        d *= 2
    h_prev = carry_ref[V7X_SUBLANES - 1:V7X_SUBLANES, :]
    h = a * h_prev + b
    carry_ref[...] = h[tt - V7X_SUBLANES:, :]
    o_ref[...] = (h * _gelu_tanh(y_ref[...].astype(F32))).astype(o_ref.dtype)


def rglru(h, conv_w, conv_b, w_a, b_a, w_i, b_i, lam, *, batch, seq, x_col0, y_col0, tt):
    nblk = w_a.shape[0]
    C = nblk * LRU_BLOCK_DIM
    nt = seq // tt
    T = batch * seq
    cb = LRU_BLOCK_DIM
    vec = lambda b, c, t: (0, c)
    blk3 = lambda b, c, t: (c, 0, 0)
    return pl.pallas_call(
        _lru_kernel,
        grid=(batch, nblk, nt),
        in_specs=[
            pl.BlockSpec((tt, cb), lambda b, c, t: (b * nt + t, x_col0 + c)),
            pl.BlockSpec((tt, cb), lambda b, c, t: (b * nt + t, y_col0 + c)),
            pl.BlockSpec((CONV_WIDTH, cb), vec),
            pl.BlockSpec((1, cb), vec),
            pl.BlockSpec((1, cb, cb), blk3),
            pl.BlockSpec((1, 1, cb), blk3),
            pl.BlockSpec((1, cb, cb), blk3),
            pl.BlockSpec((1, 1, cb), blk3),
            pl.BlockSpec((1, cb), vec),
        ],
        out_specs=pl.BlockSpec((tt, cb), lambda b, c, t: (b * nt + t, c)),
        out_shape=jax.ShapeDtypeStruct((T, C), BF16),
        scratch_shapes=[pltpu.VMEM((V7X_SUBLANES, cb), F32), pltpu.VMEM((V7X_SUBLANES, cb), F32)],
        compiler_params=_cparams(3),
        name="rglru",
    )(h, h, conv_w, conv_b.reshape(1, C), w_a, b_a.reshape(nblk, 1, cb), w_i,
      b_i.reshape(nblk, 1, cb), lam.reshape(1, C))


def _merge_kernel(oa_ref, ol_ref, wa_ref, wl_ref, ga_ref, gl_ref, bg_ref, o_ref):
    pa = jnp.dot(oa_ref[...], wa_ref[...].astype(BF16), preferred_element_type=F32)
    pb = jnp.dot(ol_ref[...], wl_ref[...].astype(BF16), preferred_element_type=F32)
    bg = bg_ref[...]
    g_a = jax.nn.sigmoid(ga_ref[...].astype(F32) + bg[0:1, :])
    g_l = jax.nn.sigmoid(gl_ref[...].astype(F32) + bg[1:2, :])
    o_ref[...] = (g_a * pa + g_l * pb).astype(o_ref.dtype)


def gated_merge(oa, ol, wa, wl, h, b_gate, *, ga_col0, gl_col0, tm, tn):
    T, K = oa.shape
    N = wa.shape[1]
    return pl.pallas_call(
        _merge_kernel,
        grid=(T // tm, N // tn),
        in_specs=[
            pl.BlockSpec((tm, K), lambda i, j: (i, 0)),
            pl.BlockSpec((tm, K), lambda i, j: (i, 0)),
            pl.BlockSpec((K, tn), lambda i, j: (0, j)),
            pl.BlockSpec((K, tn), lambda i, j: (0, j)),
            pl.BlockSpec((tm, tn), lambda i, j: (i, ga_col0 + j)),
            pl.BlockSpec((tm, tn), lambda i, j: (i, gl_col0 + j)),
            pl.BlockSpec((2, tn), lambda i, j: (0, j)),
        ],
        out_specs=pl.BlockSpec((tm, tn), lambda i, j: (i, j)),
        out_shape=jax.ShapeDtypeStruct((T, N), BF16),
        compiler_params=_cparams(2),
        name="gated_merge",
    )(oa, ol, wa, wl, h, h, b_gate)


def _xattn_kernel(y_ref, g_ref, b_ref, wq_ref, kv_ref, wo_ref, o_ref, *, scale, alpha):
    x1 = _layer_norm_rows(y_ref[...], g_ref[...], b_ref[...])
    q = jnp.dot(x1.astype(BF16), wq_ref[...], preferred_element_type=F32) * scale
    qb = q.astype(BF16)
    width = MEM_HEADS * MEM_HEAD_DIM
    outs = []
    for hd in range(MEM_HEADS):
        sl = slice(hd * MEM_HEAD_DIM, (hd + 1) * MEM_HEAD_DIM)
        k = kv_ref[:, sl]
        v = kv_ref[:, width + hd * MEM_HEAD_DIM:width + (hd + 1) * MEM_HEAD_DIM]
        s = lax.dot_general(qb[:, sl], k, (((1,), (1,)), ((), ())), preferred_element_type=F32)
        m = jnp.max(s, axis=-1, keepdims=True)
        p = jnp.exp(s - m)
        den = jnp.sum(p, axis=-1, keepdims=True)
        outs.append(jnp.dot(p.astype(BF16), v, preferred_element_type=F32) / den)
    o = jnp.concatenate(outs, axis=1).astype(BF16)
    xa = jnp.dot(o, wo_ref[...], preferred_element_type=F32)
    o_ref[...] = alpha * x1 + xa


def memory_cross_attention(y1, ln_g, ln_b, wq, kv, wo, *, seq, mem_len, tm):
    T, D = y1.shape
    per_batch = seq // tm
    width = MEM_HEADS * MEM_HEAD_DIM
    return pl.pallas_call(
        functools.partial(_xattn_kernel, scale=MEM_HEAD_DIM ** -0.5, alpha=DEEPNORM_ALPHA),
        grid=(T // tm,),
        in_specs=[
            pl.BlockSpec((tm, D), lambda i: (i, 0)),
            pl.BlockSpec((1, D), lambda i: (0, 0)),
            pl.BlockSpec((1, D), lambda i: (0, 0)),
            pl.BlockSpec((D, width), lambda i: (0, 0)),
            pl.BlockSpec((mem_len, 2 * width), lambda i: (i // per_batch, 0)),
            pl.BlockSpec((width, D), lambda i: (0, 0)),
        ],
        out_specs=pl.BlockSpec((tm, D), lambda i: (i, 0)),
        out_shape=jax.ShapeDtypeStruct((T, D), F32),
        compiler_params=_cparams(1),
        name="mem_xattn",
    )(y1, ln_g.reshape(1, D), ln_b.reshape(1, D), wq, kv, wo)


def _router_kernel(y_ref, g_ref, b_ref, wr_ref, br_ref, x2_ref, idx_ref, rank_ref, wts_ref, cnt_ref,
                   carry_ref):
    tm = y_ref.shape[0]

    @pl.when(pl.program_id(0) == 0)
    def _():
        carry_ref[...] = jnp.zeros_like(carry_ref)

    x2 = _layer_norm_rows(y_ref[...], g_ref[...], b_ref[...])
    x2_ref[...] = _pack_bf16_pairs(x2)
    logits = lax.dot_general(wr_ref[...], x2, (((1,), (1,)), ((), ())),
                             precision=lax.Precision.HIGHEST, preferred_element_type=F32)
    logits = logits + br_ref[...][:, :1]
    eidx = lax.broadcasted_iota(jnp.int32, logits.shape, 0)
    work = logits
    vals, idxs, hots = [], [], []
    for _ in range(TOP_K):
        m = jnp.max(work, axis=0, keepdims=True)
        idx = jnp.min(jnp.where(work == m, eidx, N_EXPERTS), axis=0, keepdims=True)
        hot = eidx == idx
        vals.append(m)
        idxs.append(idx)
        hots.append(hot)
        work = jnp.where(hot, -jnp.inf, work)
    exps = [jnp.exp(v - vals[0]) for v in vals]
    den = exps[0] + exps[1] + exps[2] + exps[3]
    zero_row = jnp.zeros_like(den)
    wts_ref[...] = jnp.concatenate([e / den for e in exps] + [zero_row] * (V7X_SUBLANES - TOP_K), axis=0)

    sel = jnp.zeros(logits.shape, F32)
    for hot in hots:
        sel = sel + jnp.where(hot, 1.0, 0.0)
    r_i = lax.broadcasted_iota(jnp.int32, (tm, tm), 0)
    c_i = lax.broadcasted_iota(jnp.int32, (tm, tm), 1)
    upper = jnp.where(r_i <= c_i, 1.0, 0.0).astype(BF16)
    incl = jnp.dot(sel.astype(BF16), upper, preferred_element_type=F32)
    carry = carry_ref[...][:, :1]
    excl = carry + incl - sel
    ranks = [jnp.sum(jnp.where(hot, excl, 0.0), axis=0, keepdims=True) for hot in hots]
    rank_ref[...] = jnp.concatenate(ranks, axis=0).astype(jnp.int32)
    idx_ref[...] = jnp.concatenate(idxs, axis=0)
    new_carry = carry + incl[:, tm - 1:tm]
    carry_ref[...] = jnp.broadcast_to(new_carry, carry_ref.shape)
    cnt_ref[...] = jnp.broadcast_to(new_carry, cnt_ref.shape)


def router(y2, ln_g, ln_b, w_router_t, b_router, *, tm):
    T, D = y2.shape
    E = w_router_t.shape[0]
    br = jnp.broadcast_to(b_router.reshape(E, 1), (E, V7X_LANES))
    return pl.pallas_call(
        _router_kernel,
        grid=(T // tm,),
        in_specs=[
            pl.BlockSpec((tm, D), lambda i: (i, 0)),
            pl.BlockSpec((1, D), lambda i: (0, 0)),
            pl.BlockSpec((1, D), lambda i: (0, 0)),
            pl.BlockSpec((E, D), lambda i: (0, 0)),
            pl.BlockSpec((E, V7X_LANES), lambda i: (0, 0)),
        ],
        out_specs=[
            pl.BlockSpec((tm, D // 2), lambda i: (i, 0)),
            pl.BlockSpec((TOP_K, tm), lambda i: (0, i)),
            pl.BlockSpec((TOP_K, tm), lambda i: (0, i)),
            pl.BlockSpec((V7X_SUBLANES, tm), lambda i: (0, i)),
            pl.BlockSpec((E, V7X_LANES), lambda i: (0, 0)),
        ],
        out_shape=[
            jax.ShapeDtypeStruct((T, D // 2), jnp.uint32),
            jax.ShapeDtypeStruct((TOP_K, T), jnp.int32),
            jax.ShapeDtypeStruct((TOP_K, T), jnp.int32),
            jax.ShapeDtypeStruct((V7X_SUBLANES, T), F32),
            jax.ShapeDtypeStruct((E, V7X_LANES), F32),
        ],
        scratch_shapes=[pltpu.VMEM((E, V7X_LANES), F32)],
        compiler_params=_cparams(1),
        name="router",
    )(y2, ln_g.reshape(1, D), ln_b.reshape(1, D), w_router_t, br)


def _pos_kernel(idx_ref, rank_ref, off_ref, pos_ref):
    off = off_ref[...][:, :1]
    shape = (off_ref.shape[0], idx_ref.shape[1])
    eidx = lax.broadcasted_iota(jnp.int32, shape, 0)
    rows = []
    for k in range(TOP_K):
        hot = eidx == idx_ref[k:k + 1, :]
        rows.append(jnp.sum(jnp.where(hot, off, 0.0), axis=0, keepdims=True))
    pos_ref[...] = jnp.concatenate(rows, axis=0).astype(jnp.int32) + rank_ref[...]


def grouped_positions(idx, rank, offsets_f, *, tm):
    T = idx.shape[1]
    E = offsets_f.shape[0]
    return pl.pallas_call(
        _pos_kernel,
        grid=(T // tm,),
        in_specs=[pl.BlockSpec((TOP_K, tm), lambda i: (0, i)),
                  pl.BlockSpec((TOP_K, tm), lambda i: (0, i)),
                  pl.BlockSpec((E, V7X_LANES), lambda i: (0, 0))],
        out_specs=pl.BlockSpec((TOP_K, tm), lambda i: (0, i)),
        out_shape=jax.ShapeDtypeStruct((TOP_K, T), jnp.int32),
        compiler_params=_cparams(1),
        name="grouped_pos",
    )(idx, rank, offsets_f)


def _dispatch_kernel(pos_ref, x_ref, xg_ref, sem):
    td = x_ref.shape[0]

    def row_copy(t, p):
        return pltpu.make_async_copy(x_ref.at[pl.ds(t, 1)], xg_ref.at[pl.ds(p, 1)], sem)

    def issue(t, c):
        for k in range(TOP_K):
            row_copy(t, pos_ref[k, t]).start()
        return c

    lax.fori_loop(0, td, issue, 0)

    def drain(t, c):
        for k in range(TOP_K):
            row_copy(0, 0).wait()
        return c

    lax.fori_loop(0, td, drain, 0)


def dispatch(x2, pos, *, rows, td):
    T, D = x2.shape
    return pl.pallas_call(
        _dispatch_kernel,
        grid=(T // td,),
        in_specs=[pl.BlockSpec((TOP_K, td), lambda i: (0, i), memory_space=pltpu.SMEM),
                  pl.BlockSpec((td, D), lambda i: (i, 0))],
        out_specs=pl.BlockSpec(memory_space=pl.ANY),
        out_shape=jax.ShapeDtypeStruct((rows, D), x2.dtype),
        scratch_shapes=[pltpu.SemaphoreType.DMA(())],
        compiler_params=_cparams(1),
        name="moe_dispatch",
    )(pos, x2)


def _moe_up_kernel(te_ref, tv_ref, x_ref, wg_ref, wu_ref, bg_ref, bu_ref, o_ref):
    @pl.when(tv_ref[pl.program_id(1)] > 0)
    def _():
        x_lo, x_hi = _unpack_bf16_pairs(x_ref[...])
        n = x_lo.shape[1]

        def proj(w_ref, b_ref):
            return (jnp.dot(x_lo, w_ref[0, :n, :].astype(BF16), preferred_element_type=F32)
                    + jnp.dot(x_hi, w_ref[0, n:, :].astype(BF16), preferred_element_type=F32) + b_ref[0])

        gate = proj(wg_ref, bg_ref)
        up = proj(wu_ref, bu_ref)
        gate = jnp.minimum(gate, SWIGLU_LIMIT)
        up = jnp.clip(up, -SWIGLU_LIMIT, SWIGLU_LIMIT)
        glu = gate * jax.nn.sigmoid(SWIGLU_ALPHA * gate)
        o_ref[...] = ((up + 1.0) * glu).astype(o_ref.dtype)


def moe_up(xg, w_gate_up, b_gate_up, tile_expert, tile_valid, *, tm, tf):
    R, half_d = xg.shape
    E, D, F2 = w_gate_up.shape
    assert D == 2 * half_d
    F = F2 // 2
    nf = F // tf
    nt = R // tm
    grid_spec = pltpu.PrefetchScalarGridSpec(
        num_scalar_prefetch=2,
        grid=(nf, nt),
        in_specs=[
            pl.BlockSpec((tm, half_d), lambda j, w, te, tv: (w, 0)),
            pl.BlockSpec((1, D, tf), lambda j, w, te, tv: (te[w], 0, j)),
            pl.BlockSpec((1, D, tf), lambda j, w, te, tv: (te[w], 0, nf + j)),
            pl.BlockSpec((1, 1, tf), lambda j, w, te, tv: (te[w], 0, j)),
            pl.BlockSpec((1, 1, tf), lambda j, w, te, tv: (te[w], 0, nf + j)),
        ],
        out_specs=pl.BlockSpec((tm, tf), lambda j, w, te, tv: (w, j)),
    )
    b3 = b_gate_up.reshape(E, 1, F2)
    return pl.pallas_call(
        _moe_up_kernel,
        grid_spec=grid_spec,
        out_shape=jax.ShapeDtypeStruct((R, F), BF16),
        compiler_params=_cparams(2),
        name="moe_up",
    )(tile_expert, tile_valid, xg, w_gate_up, w_gate_up, b3, b3)


def _moe_down_kernel(te_ref, tv_ref, h_ref, wd_ref, bd_ref, o_ref):
    @pl.when(tv_ref[pl.program_id(1)] > 0)
    def _():
        o_ref[...] = jnp.dot(h_ref[...], wd_ref[0].astype(BF16), preferred_element_type=F32) + bd_ref[0]


def moe_down(hg, w_down, b_down, tile_expert, tile_valid, *, tm, tn):
    R, F = hg.shape
    E, _, D = w_down.shape
    nn = D // tn
    nt = R // tm
    grid_spec = pltpu.PrefetchScalarGridSpec(
        num_scalar_prefetch=2,
        grid=(nn, nt),
        in_specs=[
            pl.BlockSpec((tm, F), lambda j, w, te, tv: (w, 0)),
            pl.BlockSpec((1, F, tn), lambda j, w, te, tv: (te[w], 0, j)),
            pl.BlockSpec((1, 1, tn), lambda j, w, te, tv: (te[w], 0, j)),
        ],
        out_specs=pl.BlockSpec((tm, tn), lambda j, w, te, tv: (w, j)),
    )
    return pl.pallas_call(
        _moe_down_kernel,
        grid_spec=grid_spec,
        out_shape=jax.ShapeDtypeStruct((R, D), F32),
        compiler_params=_cparams(2),
        name="moe_down",
    )(tile_expert, tile_valid, hg, w_down, b_down.reshape(E, 1, D))


def _combine_kernel(pos_ref, y2_ref, w_ref, g2_ref, b2_ref, g_ref, b_ref, yg_ref, o_ref, buf_ref, sem,
                    *, alpha):
    tc = y2_ref.shape[0]

    def row_copy(k, t, p):
        return pltpu.make_async_copy(yg_ref.at[pl.ds(p, 1)], buf_ref.at[k, pl.ds(t, 1)], sem)

    def issue(t, c):
        for k in range(TOP_K):
            row_copy(k, t, pos_ref[k, t]).start()
        return c

    lax.fori_loop(0, tc, issue, 0)

    def drain(t, c):
        for k in range(TOP_K):
            row_copy(k, 0, 0).wait()
        return c

    lax.fori_loop(0, tc, drain, 0)

    wt = w_ref[...]
    wt = jnp.concatenate([wt, jnp.zeros((V7X_LANES - wt.shape[0], tc), F32)], axis=0)
    w_cols = wt.T
    y = alpha * _layer_norm_rows(y2_ref[...], g2_ref[...], b2_ref[...])
    for k in range(TOP_K):
        y = y + w_cols[:, k:k + 1] * buf_ref[k]
    o_ref[...] = _layer_norm_rows(y, g_ref[...], b_ref[...])


def combine(y2, wts, pos, yg, ln2_g, ln2_b, ln_g, ln_b, *, tc):
    T, D = y2.shape
    assert tc == V7X_LANES
    return pl.pallas_call(
        functools.partial(_combine_kernel, alpha=DEEPNORM_ALPHA),
        grid=(T // tc,),
        in_specs=[
            pl.BlockSpec((TOP_K, tc), lambda i: (0, i), memory_space=pltpu.SMEM),
            pl.BlockSpec((tc, D), lambda i: (i, 0)),
            pl.BlockSpec((V7X_SUBLANES, tc), lambda i: (0, i)),
            pl.BlockSpec((1, D), lambda i: (0, 0)),
            pl.BlockSpec((1, D), lambda i: (0, 0)),
            pl.BlockSpec((1, D), lambda i: (0, 0)),
            pl.BlockSpec((1, D), lambda i: (0, 0)),
            pl.BlockSpec(memory_space=pl.ANY),
        ],
        out_specs=pl.BlockSpec((tc, D), lambda i: (i, 0)),
        out_shape=jax.ShapeDtypeStruct((T, D), F32),
        scratch_shapes=[pltpu.VMEM((TOP_K, tc, D), F32), pltpu.SemaphoreType.DMA(())],
        compiler_params=_cparams(1),
        name="moe_combine",
    )(pos, y2, wts, ln2_g.reshape(1, D), ln2_b.reshape(1, D), ln_g.reshape(1, D), ln_b.reshape(1, D), yg)


def _tile_maps(counts, *, tm, n_tiles):
    counts = counts.astype(jnp.int32)
    tiles_per = (counts + tm - 1) // tm
    tile_end = jnp.cumsum(tiles_per)
    offsets = (tile_end - tiles_per) * tm
    w = jnp.arange(n_tiles, dtype=jnp.int32)
    te = jnp.sum((w[:, None] >= tile_end[None, :]).astype(jnp.int32), axis=1)
    valid = (w < tile_end[-1]).astype(jnp.int32)
    last_e = jnp.max(jnp.where(tiles_per > 0, jnp.arange(counts.shape[0], dtype=jnp.int32), 0))
    te = jnp.where(valid > 0, te, last_e).astype(jnp.int32)
    return offsets, te, valid


def kernel(x, mem, positions, w_in, b_gate, attn_sinks, conv_w, conv_b, w_lru_a, b_lru_a, w_lru_i,
           b_lru_i, lru_lambda, w_branch_attn, w_branch_lru, w_mix_out, ln1_g, ln1_b, w_mem_q, w_mem_kv,
           w_mem_o, ln2_g, ln2_b, w_router, b_router, w_gate_up, b_gate_up, w_down, b_down, ln3_g, ln3_b):
    B, S, D = x.shape
    T = B * S
    l = 0
    n_q_heads = D // HEAD_DIM
    n_kv_heads = n_q_heads // KV_GROUP
    q_width = n_q_heads * HEAD_DIM
    kv_width = n_kv_heads * HEAD_DIM
    lru_width = w_lru_a.shape[1] * LRU_BLOCK_DIM
    col_xl = q_width + 2 * kv_width
    col_yl = col_xl + lru_width
    col_ga = col_yl + lru_width
    col_gl = col_ga + D

    xf = x.reshape(T, D)
    xb = xf.astype(BF16)

    h = matmul(xb, w_in[l].astype(BF16), tm=1024, tn=512, out_dtype=BF16, name="in_proj")

    half = ROPE_DIM // 2
    inv_freq = 1.0 / (ROPE_THETA ** (np.arange(0, ROPE_DIM, 2, dtype=np.float32) / ROPE_DIM))
    freq_row = np.zeros((1, HEAD_DIM), np.float32)
    freq_row[0, :half] = inv_freq
    freq_row[0, half:ROPE_DIM] = inv_freq
    pos_col = positions.astype(F32).reshape(T, 1)
    qk = rope(h, pos_col, jnp.asarray(freq_row), n_q_heads=n_q_heads, n_k_heads=n_kv_heads,
              scale=HEAD_DIM ** -0.5, tm=256)
    sink_rows = jnp.broadcast_to(
        jnp.repeat(attn_sinks[l].astype(F32), ATTN_BLOCK).reshape(n_kv_heads, KV_GROUP * ATTN_BLOCK, 1),
        (n_kv_heads, KV_GROUP * ATTN_BLOCK, V7X_LANES))
    o_attn = sliding_window_attention(
        qk, h, jnp.asarray(_swa_bias()), sink_rows, batch=B, seq=S, n_kv_heads=n_kv_heads,
        k_col0=q_width // HEAD_DIM, v_col0=(q_width + kv_width) // HEAD_DIM)

    o_lru = rglru(h, conv_w[l], conv_b[l], w_lru_a[l], b_lru_a[l], w_lru_i[l], b_lru_i[l], lru_lambda[l],
                  batch=B, seq=S, x_col0=col_xl // LRU_BLOCK_DIM, y_col0=col_yl // LRU_BLOCK_DIM,
                  tt=LRU_TIME_TILE)

    tn_merge = 512
    mixed = gated_merge(o_attn, o_lru, w_branch_attn[l].astype(BF16), w_branch_lru[l].astype(BF16), h,
                        b_gate[l], ga_col0=col_ga // tn_merge, gl_col0=col_gl // tn_merge, tm=512, tn=tn_merge)
    y1 = matmul(mixed, w_mix_out[l].astype(BF16), tm=1024, tn=512, out_dtype=F32, res=xf,
                alpha=DEEPNORM_ALPHA, name="mix_out")

    M = mem.shape[1]
    kv = matmul(mem.reshape(B * M, D).astype(BF16), w_mem_kv[l].astype(BF16), tm=B * M, tn=512,
                out_dtype=BF16, name="mem_kv")
    y2 = memory_cross_attention(y1, ln1_g[l], ln1_b[l], w_mem_q[l].astype(BF16), kv,
                                w_mem_o[l].astype(BF16), seq=S, mem_len=M, tm=256)

    x2p, idx, rank, wts, cnt = router(y2, ln2_g[l], ln2_b[l], w_router[l].T, b_router[l], tm=256)
    n_tiles = (T * TOP_K) // MOE_ROW_TILE + N_EXPERTS
    offsets, tile_expert, tile_valid = _tile_maps(cnt[:, 0], tm=MOE_ROW_TILE, n_tiles=n_tiles)
    offsets_f = jnp.broadcast_to(offsets.astype(F32)[:, None], (N_EXPERTS, V7X_LANES))
    pos = grouped_positions(idx, rank, offsets_f, tm=2048)

    rows = n_tiles * MOE_ROW_TILE
    xg = dispatch(x2p, pos, rows=rows, td=DISPATCH_TILE)
    hg = moe_up(xg, w_gate_up[l], b_gate_up[l], tile_expert, tile_valid, tm=MOE_ROW_TILE, tf=MOE_F_TILE)
    yg = moe_down(hg, w_down[l], b_down[l], tile_expert, tile_valid, tm=MOE_ROW_TILE, tn=MOE_N_TILE)
    out = combine(y2, wts, pos, yg, ln2_g[l], ln2_b[l], ln3_g[l], ln3_b[l], tc=DISPATCH_TILE)
    return out.reshape(B, S, D)
```

```python
import functools

import numpy as np
import jax
import jax.numpy as jnp
from jax import lax
from jax.experimental import pallas as pl
from jax.experimental.pallas import tpu as pltpu

F32 = jnp.float32
BF16 = jnp.bfloat16

HEAD_DIM = 128
KV_GROUP = 4
ATTN_BLOCK = 128
ROPE_DIM = HEAD_DIM // 4
ROPE_THETA = 500000.0
LRU_BLOCK_DIM = 256
CONV_WIDTH = 4
LRU_C = 8.0
MEM_HEADS = 4
MEM_HEAD_DIM = 128
N_EXPERTS = 32
TOP_K = 4
SWIGLU_LIMIT = 7.0
SWIGLU_ALPHA = 1.702
LN_EPS = 1e-5
DEPTH = 1
DEEPNORM_ALPHA = (2.0 * DEPTH) ** 0.25

V7X_LANES = 128
V7X_SUBLANES = 8
V7X_VMEM_LIMIT_BYTES = 56 * 1024 * 1024

MOE_ROW_TILE = 512
MOE_F_TILE = 512
MOE_N_TILE = 1024
LRU_TIME_TILE = 512
DISPATCH_TILE = 128


def _cparams(n_axes):
    return pltpu.CompilerParams(dimension_semantics=("arbitrary",) * n_axes,
                                vmem_limit_bytes=V7X_VMEM_LIMIT_BYTES)


def _layer_norm_rows(y, g, b):
    mu = jnp.mean(y, axis=-1, keepdims=True)
    d = y - mu
    var = jnp.mean(d * d, axis=-1, keepdims=True)
    return d * lax.rsqrt(var + LN_EPS) * g + b


def _pack_bf16_pairs(x):
    n = x.shape[1] // 2
    u = pltpu.bitcast(x, jnp.uint32)
    r = (u + jnp.uint32(0x7FFF) + ((u >> 16) & jnp.uint32(1))) >> 16
    return (r[:, n:] << 16) | r[:, :n]


def _unpack_bf16_pairs(w):
    lo = pltpu.bitcast(w << 16, F32).astype(BF16)
    hi = pltpu.bitcast(w & jnp.uint32(0xFFFF0000), F32).astype(BF16)
    return lo, hi


def _mm_kernel(a_ref, b_ref, o_ref):
    o_ref[...] = jnp.dot(a_ref[...].astype(BF16), b_ref[...].astype(BF16),
                         preferred_element_type=F32).astype(o_ref.dtype)


def _mm_res_kernel(a_ref, b_ref, r_ref, o_ref, *, alpha):
    acc = jnp.dot(a_ref[...].astype(BF16), b_ref[...].astype(BF16), preferred_element_type=F32)
    o_ref[...] = (alpha * r_ref[...] + acc).astype(o_ref.dtype)


def matmul(a, b, *, tm, tn, out_dtype, res=None, alpha=1.0, name="matmul"):
    M, K = a.shape
    N = b.shape[1]
    assert M % tm == 0 and N % tn == 0
    in_specs = [pl.BlockSpec((tm, K), lambda i, j: (i, 0)),
                pl.BlockSpec((K, tn), lambda i, j: (0, j))]
    args = [a, b]
    if res is None:
        body = _mm_kernel
    else:
        body = functools.partial(_mm_res_kernel, alpha=alpha)
        in_specs.append(pl.BlockSpec((tm, tn), lambda i, j: (i, j)))
        args.append(res)
    return pl.pallas_call(
        body,
        grid=(M // tm, N // tn),
        in_specs=in_specs,
        out_specs=pl.BlockSpec((tm, tn), lambda i, j: (i, j)),
        out_shape=jax.ShapeDtypeStruct((M, N), out_dtype),
        compiler_params=_cparams(2),
        name=name,
    )(*args)


def _rope_kernel(pos_ref, freq_ref, h_ref, o_ref, *, n_q_heads, scale):
    ang = pos_ref[...] * freq_ref[...]
    lane = lax.broadcasted_iota(jnp.int32, ang.shape, 1)
    cos = jnp.cos(ang)
    sin = jnp.sin(ang)
    half = ROPE_DIM // 2
    c_mul = jnp.where(lane < ROPE_DIM, cos, 1.0)
    s_lo = jnp.where(lane < half, -sin, 0.0)
    s_hi = jnp.where((lane >= half) & (lane < ROPE_DIM), sin, 0.0)
    n_heads = h_ref.shape[1] // HEAD_DIM
    for hd in range(n_heads):
        t = h_ref[:, hd * HEAD_DIM:(hd + 1) * HEAD_DIM].astype(F32)
        up = pltpu.roll(t, HEAD_DIM - half, axis=1)
        dn = pltpu.roll(t, half, axis=1)
        r = t * c_mul + up * s_lo + dn * s_hi
        if hd < n_q_heads:
            r = r * scale
        o_ref[:, hd * HEAD_DIM:(hd + 1) * HEAD_DIM] = r.astype(o_ref.dtype)


def rope(h, pos_col, freq_row, *, n_q_heads, n_k_heads, scale, tm):
    T = h.shape[0]
    width = (n_q_heads + n_k_heads) * HEAD_DIM
    return pl.pallas_call(
        functools.partial(_rope_kernel, n_q_heads=n_q_heads, scale=scale),
        grid=(T // tm,),
        in_specs=[pl.BlockSpec((tm, 1), lambda i: (i, 0)),
                  pl.BlockSpec((1, HEAD_DIM), lambda i: (0, 0)),
                  pl.BlockSpec((tm, width), lambda i: (i, 0))],
        out_specs=pl.BlockSpec((tm, width), lambda i: (i, 0)),
        out_shape=jax.ShapeDtypeStruct((T, width), BF16),
        compiler_params=_cparams(1),
        name="rope",
    )(pos_col, freq_row, h)


def _swa_kernel(bias_ref, sink_ref, q_ref, kc_ref, kp_ref, vc_ref, vp_ref, o_ref):
    bias = bias_ref[0]
    qw = KV_GROUP * HEAD_DIM
    n_kv_heads = kc_ref.shape[1] // HEAD_DIM
    for g in range(n_kv_heads):
        kv_cols = slice(g * HEAD_DIM, (g + 1) * HEAD_DIM)
        q4 = jnp.concatenate(
            [q_ref[:, g * qw + i * HEAD_DIM:g * qw + (i + 1) * HEAD_DIM] for i in range(KV_GROUP)], axis=0)
        k = jnp.concatenate([kp_ref[:, kv_cols], kc_ref[:, kv_cols]], axis=0)
        v = jnp.concatenate([vp_ref[:, kv_cols], vc_ref[:, kv_cols]], axis=0)
        s = lax.dot_general(q4, k, (((1,), (1,)), ((), ())), preferred_element_type=F32) + bias
        sink = sink_ref[g][:, :1]
        m = jnp.maximum(jnp.max(s, axis=-1, keepdims=True), sink)
        p = jnp.exp(s - m)
        denom = jnp.sum(p, axis=-1, keepdims=True) + jnp.exp(sink - m)
        o = jnp.dot(p.astype(BF16), v, preferred_element_type=F32) / denom
        for i in range(KV_GROUP):
            o_ref[:, g * qw + i * HEAD_DIM:g * qw + (i + 1) * HEAD_DIM] = (
                o[i * ATTN_BLOCK:(i + 1) * ATTN_BLOCK].astype(o_ref.dtype))


def sliding_window_attention(qk, h, bias, sink_rows, *, batch, seq, n_kv_heads, k_col0, v_col0):
    nb = seq // ATTN_BLOCK
    T = batch * seq
    q_width = n_kv_heads * KV_GROUP * HEAD_DIM
    kv_width = n_kv_heads * HEAD_DIM

    def cur(b, n):
        return b * nb + n

    def prev(b, n):
        return b * nb + jnp.maximum(n - 1, 0)

    return pl.pallas_call(
        _swa_kernel,
        grid=(batch, nb),
        in_specs=[
            pl.BlockSpec((1, KV_GROUP * ATTN_BLOCK, 2 * ATTN_BLOCK), lambda b, n: (jnp.minimum(n, 1), 0, 0)),
            pl.BlockSpec((n_kv_heads, KV_GROUP * ATTN_BLOCK, V7X_LANES), lambda b, n: (0, 0, 0)),
            pl.BlockSpec((ATTN_BLOCK, q_width), lambda b, n: (cur(b, n), 0)),
            pl.BlockSpec((ATTN_BLOCK, kv_width), lambda b, n: (cur(b, n), k_col0)),
            pl.BlockSpec((ATTN_BLOCK, kv_width), lambda b, n: (prev(b, n), k_col0)),
            pl.BlockSpec((ATTN_BLOCK, kv_width), lambda b, n: (cur(b, n), v_col0)),
            pl.BlockSpec((ATTN_BLOCK, kv_width), lambda b, n: (prev(b, n), v_col0)),
        ],
        out_specs=pl.BlockSpec((ATTN_BLOCK, q_width), lambda b, n: (cur(b, n), 0)),
        out_shape=jax.ShapeDtypeStruct((T, q_width), BF16),
        compiler_params=_cparams(2),
        name="swa",
    )(bias, sink_rows, qk, qk, qk, h, h)


def _swa_bias():
    qi = np.arange(ATTN_BLOCK)[:, None]
    si = np.arange(2 * ATTN_BLOCK)[None, :]
    delta = ATTN_BLOCK + qi - si
    band = (delta >= 0) & (delta < ATTN_BLOCK)
    first = band & (si >= ATTN_BLOCK)
    m = np.stack([first, band]).astype(np.float32)
    bias = np.where(m > 0, 0.0, -np.inf).astype(np.float32)
    return np.tile(bias, (1, KV_GROUP, 1))


def _gelu_tanh(y):
    return 0.5 * y * (1.0 + jnp.tanh(0.7978845608028654 * (y + 0.044715 * (y * y * y))))


def _lru_kernel(x_ref, y_ref, cw_ref, cb_ref, wa_ref, ba_ref, wi_ref, bi_ref, lam_ref, o_ref,
                tail_ref, carry_ref):
    tt = x_ref.shape[0]

    @pl.when(pl.program_id(2) == 0)
    def _():
        tail_ref[...] = jnp.zeros_like(tail_ref)
        carry_ref[...] = jnp.zeros_like(carry_ref)

    x = x_ref[...].astype(F32)
    ext = jnp.concatenate([tail_ref[...], x], axis=0)
    tail_ref[...] = x[tt - V7X_SUBLANES:, :]
    cw = cw_ref[...]
    xc = cb_ref[...] + cw[CONV_WIDTH - 1:CONV_WIDTH, :] * x
    for j in range(1, CONV_WIDTH):
        sh = pltpu.roll(ext, j, axis=0)[V7X_SUBLANES:, :]
        xc = xc + cw[CONV_WIDTH - 1 - j:CONV_WIDTH - j, :] * sh
    xcb = xc.astype(BF16)
    r = jax.nn.sigmoid(jnp.dot(xcb, wa_ref[0].astype(BF16), preferred_element_type=F32) + ba_ref[0])
    gi = jax.nn.sigmoid(jnp.dot(xcb, wi_ref[0].astype(BF16), preferred_element_type=F32) + bi_ref[0])
    z = -lam_ref[...]
    softplus = jnp.maximum(z, 0.0) + jnp.log(1.0 + jnp.exp(-jnp.abs(z)))
    log_a = (-LRU_C) * r * softplus
    a = jnp.exp(log_a)
    b = jnp.sqrt(1.0 - a * a) * (gi * xc)

    row = lax.broadcasted_iota(jnp.int32, a.shape, 0)
    d = 1
    while d < tt:
        if d < V7X_SUBLANES:
            a_sh = jnp.where(row >= d, pltpu.roll(a, d, axis=0), 1.0)
            b_sh = jnp.where(row >= d, pltpu.roll(b, d, axis=0), 0.0)
            b = a * b_sh + b
            a = a * a_sh
        else:
            b = jnp.concatenate([b[:d], a[d:] * b[:-d] + b[d:]], axis=0)
            a = jnp.concatenate([a[:d], a[d:] * a[:-d]], axis=0)
        d *= 2
    h_prev = carry_ref[V7X_SUBLANES - 1:V7X_SUBLANES, :]
    h = a * h_prev + b
    carry_ref[...] = h[tt - V7X_SUBLANES:, :]
    o_ref[...] = (h * _gelu_tanh(y_ref[...].astype(F32))).astype(o_ref.dtype)


def rglru(h, conv_w, conv_b, w_a, b_a, w_i, b_i, lam, *, batch, seq, x_col0, y_col0, tt):
    nblk = w_a.shape[0]
    C = nblk * LRU_BLOCK_DIM
    nt = seq // tt
    T = batch * seq
    cb = LRU_BLOCK_DIM
    vec = lambda b, c, t: (0, c)
    blk3 = lambda b, c, t: (c, 0, 0)
    return pl.pallas_call(
        _lru_kernel,
        grid=(batch, nblk, nt),
        in_specs=[
            pl.BlockSpec((tt, cb), lambda b, c, t: (b * nt + t, x_col0 + c)),
            pl.BlockSpec((tt, cb), lambda b, c, t: (b * nt + t, y_col0 + c)),
            pl.BlockSpec((CONV_WIDTH, cb), vec),
            pl.BlockSpec((1, cb), vec),
            pl.BlockSpec((1, cb, cb), blk3),
            pl.BlockSpec((1, 1, cb), blk3),
            pl.BlockSpec((1, cb, cb), blk3),
            pl.BlockSpec((1, 1, cb), blk3),
            pl.BlockSpec((1, cb), vec),
        ],
        out_specs=pl.BlockSpec((tt, cb), lambda b, c, t: (b * nt + t, c)),
        out_shape=jax.ShapeDtypeStruct((T, C), BF16),
        scratch_shapes=[pltpu.VMEM((V7X_SUBLANES, cb), F32), pltpu.VMEM((V7X_SUBLANES, cb), F32)],
        compiler_params=_cparams(3),
        name="rglru",
    )(h, h, conv_w, conv_b.reshape(1, C), w_a, b_a.reshape(nblk, 1, cb), w_i,
      b_i.reshape(nblk, 1, cb), lam.reshape(1, C))


def _merge_kernel(oa_ref, ol_ref, wa_ref, wl_ref, ga_ref, gl_ref, bg_ref, o_ref):
    pa = jnp.dot(oa_ref[...], wa_ref[...].astype(BF16), preferred_element_type=F32)
    pb = jnp.dot(ol_ref[...], wl_ref[...].astype(BF16), preferred_element_type=F32)
    bg = bg_ref[...]
    g_a = jax.nn.sigmoid(ga_ref[...].astype(F32) + bg[0:1, :])
    g_l = jax.nn.sigmoid(gl_ref[...].astype(F32) + bg[1:2, :])
    o_ref[...] = (g_a * pa + g_l * pb).astype(o_ref.dtype)


def gated_merge(oa, ol, wa, wl, h, b_gate, *, ga_col0, gl_col0, tm, tn):
    T, K = oa.shape
    N = wa.shape[1]
    return pl.pallas_call(
        _merge_kernel,
        grid=(T // tm, N // tn),
        in_specs=[
            pl.BlockSpec((tm, K), lambda i, j: (i, 0)),
            pl.BlockSpec((tm, K), lambda i, j: (i, 0)),
            pl.BlockSpec((K, tn), lambda i, j: (0, j)),
            pl.BlockSpec((K, tn), lambda i, j: (0, j)),
            pl.BlockSpec((tm, tn), lambda i, j: (i, ga_col0 + j)),
            pl.BlockSpec((tm, tn), lambda i, j: (i, gl_col0 + j)),
            pl.BlockSpec((2, tn), lambda i, j: (0, j)),
        ],
        out_specs=pl.BlockSpec((tm, tn), lambda i, j: (i, j)),
        out_shape=jax.ShapeDtypeStruct((T, N), BF16),
        compiler_params=_cparams(2),
        name="gated_merge",
    )(oa, ol, wa, wl, h, h, b_gate)


def _xattn_kernel(y_ref, g_ref, b_ref, wq_ref, kv_ref, wo_ref, o_ref, *, scale, alpha):
    x1 = _layer_norm_rows(y_ref[...], g_ref[...], b_ref[...])
    q = jnp.dot(x1.astype(BF16), wq_ref[...], preferred_element_type=F32) * scale
    qb = q.astype(BF16)
    width = MEM_HEADS * MEM_HEAD_DIM
    outs = []
    for hd in range(MEM_HEADS):
        sl = slice(hd * MEM_HEAD_DIM, (hd + 1) * MEM_HEAD_DIM)
        k = kv_ref[:, sl]
        v = kv_ref[:, width + hd * MEM_HEAD_DIM:width + (hd + 1) * MEM_HEAD_DIM]
        s = lax.dot_general(qb[:, sl], k, (((1,), (1,)), ((), ())), preferred_element_type=F32)
        m = jnp.max(s, axis=-1, keepdims=True)
        p = jnp.exp(s - m)
        den = jnp.sum(p, axis=-1, keepdims=True)
        outs.append(jnp.dot(p.astype(BF16), v, preferred_element_type=F32) / den)
    o = jnp.concatenate(outs, axis=1).astype(BF16)
    xa = jnp.dot(o, wo_ref[...], preferred_element_type=F32)
    o_ref[...] = alpha * x1 + xa


def memory_cross_attention(y1, ln_g, ln_b, wq, kv, wo, *, seq, mem_len, tm):
    T, D = y1.shape
    per_batch = seq // tm
    width = MEM_HEADS * MEM_HEAD_DIM
    return pl.pallas_call(
        functools.partial(_xattn_kernel, scale=MEM_HEAD_DIM ** -0.5, alpha=DEEPNORM_ALPHA),
        grid=(T // tm,),
        in_specs=[
            pl.BlockSpec((tm, D), lambda i: (i, 0)),
            pl.BlockSpec((1, D), lambda i: (0, 0)),
            pl.BlockSpec((1, D), lambda i: (0, 0)),
            pl.BlockSpec((D, width), lambda i: (0, 0)),
            pl.BlockSpec((mem_len, 2 * width), lambda i: (i // per_batch, 0)),
            pl.BlockSpec((width, D), lambda i: (0, 0)),
        ],
        out_specs=pl.BlockSpec((tm, D), lambda i: (i, 0)),
        out_shape=jax.ShapeDtypeStruct((T, D), F32),
        compiler_params=_cparams(1),
        name="mem_xattn",
    )(y1, ln_g.reshape(1, D), ln_b.reshape(1, D), wq, kv, wo)


def _router_kernel(y_ref, g_ref, b_ref, wr_ref, br_ref, x2_ref, idx_ref, rank_ref, wts_ref, cnt_ref,
                   carry_ref):
    tm = y_ref.shape[0]

    @pl.when(pl.program_id(0) == 0)
    def _():
        carry_ref[...] = jnp.zeros_like(carry_ref)

    x2 = _layer_norm_rows(y_ref[...], g_ref[...], b_ref[...])
    x2_ref[...] = _pack_bf16_pairs(x2)
    logits = lax.dot_general(wr_ref[...], x2, (((1,), (1,)), ((), ())),
                             precision=lax.Precision.HIGHEST, preferred_element_type=F32)
    logits = logits + br_ref[...][:, :1]
    eidx = lax.broadcasted_iota(jnp.int32, logits.shape, 0)
    work = logits
    vals, idxs, hots = [], [], []
    for _ in range(TOP_K):
        m = jnp.max(work, axis=0, keepdims=True)
        idx = jnp.min(jnp.where(work == m, eidx, N_EXPERTS), axis=0, keepdims=True)
        hot = eidx == idx
        vals.append(m)
        idxs.append(idx)
        hots.append(hot)
        work = jnp.where(hot, -jnp.inf, work)
    exps = [jnp.exp(v - vals[0]) for v in vals]
    den = exps[0] + exps[1] + exps[2] + exps[3]
    zero_row = jnp.zeros_like(den)
    wts_ref[...] = jnp.concatenate([e / den for e in exps] + [zero_row] * (V7X_SUBLANES - TOP_K), axis=0)

    sel = jnp.zeros(logits.shape, F32)
    for hot in hots:
        sel = sel + jnp.where(hot, 1.0, 0.0)
    r_i = lax.broadcasted_iota(jnp.int32, (tm, tm), 0)
    c_i = lax.broadcasted_iota(jnp.int32, (tm, tm), 1)
    upper = jnp.where(r_i <= c_i, 1.0, 0.0).astype(BF16)
    incl = jnp.dot(sel.astype(BF16), upper, preferred_element_type=F32)
    carry = carry_ref[...][:, :1]
    excl = carry + incl - sel
    ranks = [jnp.sum(jnp.where(hot, excl, 0.0), axis=0, keepdims=True) for hot in hots]
    rank_ref[...] = jnp.concatenate(ranks, axis=0).astype(jnp.int32)
    idx_ref[...] = jnp.concatenate(idxs, axis=0)
    new_carry = carry + incl[:, tm - 1:tm]
    carry_ref[...] = jnp.broadcast_to(new_carry, carry_ref.shape)
    cnt_ref[...] = jnp.broadcast_to(new_carry, cnt_ref.shape)


def router(y2, ln_g, ln_b, w_router_t, b_router, *, tm):
    T, D = y2.shape
    E = w_router_t.shape[0]
    br = jnp.broadcast_to(b_router.reshape(E, 1), (E, V7X_LANES))
    return pl.pallas_call(
        _router_kernel,
        grid=(T // tm,),
        in_specs=[
            pl.BlockSpec((tm, D), lambda i: (i, 0)),
            pl.BlockSpec((1, D), lambda i: (0, 0)),
            pl.BlockSpec((1, D), lambda i: (0, 0)),
            pl.BlockSpec((E, D), lambda i: (0, 0)),
            pl.BlockSpec((E, V7X_LANES), lambda i: (0, 0)),
        ],
        out_specs=[
            pl.BlockSpec((tm, D // 2), lambda i: (i, 0)),
            pl.BlockSpec((TOP_K, tm), lambda i: (0, i)),
            pl.BlockSpec((TOP_K, tm), lambda i: (0, i)),
            pl.BlockSpec((V7X_SUBLANES, tm), lambda i: (0, i)),
            pl.BlockSpec((E, V7X_LANES), lambda i: (0, 0)),
        ],
        out_shape=[
            jax.ShapeDtypeStruct((T, D // 2), jnp.uint32),
            jax.ShapeDtypeStruct((TOP_K, T), jnp.int32),
            jax.ShapeDtypeStruct((TOP_K, T), jnp.int32),
            jax.ShapeDtypeStruct((V7X_SUBLANES, T), F32),
            jax.ShapeDtypeStruct((E, V7X_LANES), F32),
        ],
        scratch_shapes=[pltpu.VMEM((E, V7X_LANES), F32)],
        compiler_params=_cparams(1),
        name="router",
    )(y2, ln_g.reshape(1, D), ln_b.reshape(1, D), w_router_t, br)


def _pos_kernel(idx_ref, rank_ref, off_ref, pos_ref):
    off = off_ref[...][:, :1]
    shape = (off_ref.shape[0], idx_ref.shape[1])
    eidx = lax.broadcasted_iota(jnp.int32, shape, 0)
    rows = []
    for k in range(TOP_K):
        hot = eidx == idx_ref[k:k + 1, :]
        rows.append(jnp.sum(jnp.where(hot, off, 0.0), axis=0, keepdims=True))
    pos_ref[...] = jnp.concatenate(rows, axis=0).astype(jnp.int32) + rank_ref[...]


def grouped_positions(idx, rank, offsets_f, *, tm):
    T = idx.shape[1]
    E = offsets_f.shape[0]
    return pl.pallas_call(
        _pos_kernel,
        grid=(T // tm,),
        in_specs=[pl.BlockSpec((TOP_K, tm), lambda i: (0, i)),
                  pl.BlockSpec((TOP_K, tm), lambda i: (0, i)),
                  pl.BlockSpec((E, V7X_LANES), lambda i: (0, 0))],
        out_specs=pl.BlockSpec((TOP_K, tm), lambda i: (0, i)),
        out_shape=jax.ShapeDtypeStruct((TOP_K, T), jnp.int32),
        compiler_params=_cparams(1),
        name="grouped_pos",
    )(idx, rank, offsets_f)


def _dispatch_kernel(pos_ref, x_ref, xg_ref, sem):
    td = x_ref.shape[0]

    def row_copy(t, p):
        return pltpu.make_async_copy(x_ref.at[pl.ds(t, 1)], xg_ref.at[pl.ds(p, 1)], sem)

    def issue(t, c):
        for k in range(TOP_K):
            row_copy(t, pos_ref[k, t]).start()
        return c

    lax.fori_loop(0, td, issue, 0)

    def drain(t, c):
        for k in range(TOP_K):
            row_copy(0, 0).wait()
        return c

    lax.fori_loop(0, td, drain, 0)


def dispatch(x2, pos, *, rows, td):
    T, D = x2.shape
    return pl.pallas_call(
        _dispatch_kernel,
        grid=(T // td,),
        in_specs=[pl.BlockSpec((TOP_K, td), lambda i: (0, i), memory_space=pltpu.SMEM),
                  pl.BlockSpec((td, D), lambda i: (i, 0))],
        out_specs=pl.BlockSpec(memory_space=pl.ANY),
        out_shape=jax.ShapeDtypeStruct((rows, D), x2.dtype),
        scratch_shapes=[pltpu.SemaphoreType.DMA(())],
        compiler_params=_cparams(1),
        name="moe_dispatch",
    )(pos, x2)


def _moe_up_kernel(te_ref, tv_ref, tr_ref, x_ref, wg_ref, wu_ref, bg_ref, bu_ref, o_ref):
    @pl.when(tv_ref[pl.program_id(1)] > 0)
    def _():
        x_lo, x_hi = _unpack_bf16_pairs(x_ref[...])
        n = x_lo.shape[1]

        def proj(w_ref, b_ref):
            return (jnp.dot(x_lo, w_ref[0, :n, :].astype(BF16), preferred_element_type=F32)
                    + jnp.dot(x_hi, w_ref[0, n:, :].astype(BF16), preferred_element_type=F32) + b_ref[0])

        gate = proj(wg_ref, bg_ref)
        up = proj(wu_ref, bu_ref)
        gate = jnp.minimum(gate, SWIGLU_LIMIT)
        up = jnp.clip(up, -SWIGLU_LIMIT, SWIGLU_LIMIT)
        glu = gate * jax.nn.sigmoid(SWIGLU_ALPHA * gate)
        o_ref[...] = ((up + 1.0) * glu).astype(o_ref.dtype)


def moe_up(xg, w_gate_up, b_gate_up, tile_maps, *, tm, tf):
    R, half_d = xg.shape
    E, D, F2 = w_gate_up.shape
    assert D == 2 * half_d
    F = F2 // 2
    nf = F // tf
    nt = R // tm
    grid_spec = pltpu.PrefetchScalarGridSpec(
        num_scalar_prefetch=3,
        grid=(nf, nt),
        in_specs=[
            pl.BlockSpec((tm, half_d), lambda j, w, te, tv, tr: (tr[w], 0)),
            pl.BlockSpec((1, D, tf), lambda j, w, te, tv, tr: (te[w], 0, j)),
            pl.BlockSpec((1, D, tf), lambda j, w, te, tv, tr: (te[w], 0, nf + j)),
            pl.BlockSpec((1, 1, tf), lambda j, w, te, tv, tr: (te[w], 0, j)),
            pl.BlockSpec((1, 1, tf), lambda j, w, te, tv, tr: (te[w], 0, nf + j)),
        ],
        out_specs=pl.BlockSpec((tm, tf), lambda j, w, te, tv, tr: (tr[w], j)),
    )
    b3 = b_gate_up.reshape(E, 1, F2)
    return pl.pallas_call(
        _moe_up_kernel,
        grid_spec=grid_spec,
        out_shape=jax.ShapeDtypeStruct((R, F), BF16),
        compiler_params=_cparams(2),
        name="moe_up",
    )(*tile_maps, xg, w_gate_up, w_gate_up, b3, b3)


def _moe_down_kernel(te_ref, tv_ref, tr_ref, h_ref, wlo_ref, whi_ref, blo_ref, bhi_ref, o_ref):
    @pl.when(tv_ref[pl.program_id(1)] > 0)
    def _():
        h = h_ref[...]
        lo = jnp.dot(h, wlo_ref[0].astype(BF16), preferred_element_type=F32) + blo_ref[0]
        hi = jnp.dot(h, whi_ref[0].astype(BF16), preferred_element_type=F32) + bhi_ref[0]
        o_ref[...] = _pack_bf16_pairs(jnp.concatenate([lo, hi], axis=1))


def moe_down(hg, w_down, b_down, tile_maps, *, tm, tn):
    R, F = hg.shape
    E, _, D = w_down.shape
    half = D // 2
    nn = half // tn
    nt = R // tm
    grid_spec = pltpu.PrefetchScalarGridSpec(
        num_scalar_prefetch=3,
        grid=(nn, nt),
        in_specs=[
            pl.BlockSpec((tm, F), lambda j, w, te, tv, tr: (tr[w], 0)),
            pl.BlockSpec((1, F, tn), lambda j, w, te, tv, tr: (te[w], 0, j)),
            pl.BlockSpec((1, F, tn), lambda j, w, te, tv, tr: (te[w], 0, nn + j)),
            pl.BlockSpec((1, 1, tn), lambda j, w, te, tv, tr: (te[w], 0, j)),
            pl.BlockSpec((1, 1, tn), lambda j, w, te, tv, tr: (te[w], 0, nn + j)),
        ],
        out_specs=pl.BlockSpec((tm, tn), lambda j, w, te, tv, tr: (tr[w], j)),
    )
    b3 = b_down.reshape(E, 1, D)
    return pl.pallas_call(
        _moe_down_kernel,
        grid_spec=grid_spec,
        out_shape=jax.ShapeDtypeStruct((R, half), jnp.uint32),
        compiler_params=_cparams(2),
        name="moe_down",
    )(*tile_maps, hg, w_down, w_down, b3, b3)


COMBINE_ROW_CHUNK = 32


def _combine_kernel(pos_ref, pos_next_ref, y2_ref, w_ref, g2_ref, b2_ref, g_ref, b_ref, yg_ref, o_ref,
                    buf_ref, wcol_ref, sem, *, alpha):
    tc = y2_ref.shape[0]
    i = pl.program_id(0)
    slot = lax.rem(i, 2)

    def row_copy(s, k, t, p):
        return pltpu.make_async_copy(yg_ref.at[pl.ds(p, 1)], buf_ref.at[s, k, pl.ds(t, 1)], sem.at[s])

    def gather(p_ref, s):
        def issue(t, c):
            for k in range(TOP_K):
                row_copy(s, k, t, p_ref[k, t]).start()
            return c
        lax.fori_loop(0, tc, issue, 0)

    @pl.when(i == 0)
    def _():
        gather(pos_ref, 0)

    @pl.when(i + 1 < pl.num_programs(0))
    def _():
        gather(pos_next_ref, 1 - slot)

    def drain(t, c):
        for k in range(TOP_K):
            row_copy(slot, k, 0, 0).wait()
        return c

    lax.fori_loop(0, tc, drain, 0)

    wt = w_ref[...]
    wt = jnp.concatenate([wt, jnp.zeros((V7X_LANES - wt.shape[0], tc), F32)], axis=0)
    wcol_ref[...] = wt.T

    def chunk(c, carry):
        rows = pl.ds(pl.multiple_of(c * COMBINE_ROW_CHUNK, COMBINE_ROW_CHUNK), COMBINE_ROW_CHUNK)
        y = alpha * _layer_norm_rows(y2_ref[rows, :], g2_ref[...], b2_ref[...])
        w_cols = wcol_ref[rows, :]
        for k in range(TOP_K):
            word = buf_ref[slot, k, rows, :]
            lo = pltpu.bitcast(word << 16, F32)
            hi = pltpu.bitcast(word & jnp.uint32(0xFFFF0000), F32)
            y = y + w_cols[:, k:k + 1] * jnp.concatenate([lo, hi], axis=1)
        o_ref[rows, :] = _layer_norm_rows(y, g_ref[...], b_ref[...])
        return carry

    lax.fori_loop(0, tc // COMBINE_ROW_CHUNK, chunk, 0)


def combine(y2, wts, pos, yg, ln2_g, ln2_b, ln_g, ln_b, *, tc):
    T, D = y2.shape
    assert tc == V7X_LANES and yg.shape[1] * 2 == D
    n_steps = T // tc
    return pl.pallas_call(
        functools.partial(_combine_kernel, alpha=DEEPNORM_ALPHA),
        grid=(n_steps,),
        in_specs=[
            pl.BlockSpec((TOP_K, tc), lambda i: (0, i), memory_space=pltpu.SMEM),
            pl.BlockSpec((TOP_K, tc), lambda i: (0, jnp.minimum(i + 1, n_steps - 1)), memory_space=pltpu.SMEM),
            pl.BlockSpec((tc, D), lambda i: (i, 0)),
            pl.BlockSpec((V7X_SUBLANES, tc), lambda i: (0, i)),
            pl.BlockSpec((1, D), lambda i: (0, 0)),
            pl.BlockSpec((1, D), lambda i: (0, 0)),
            pl.BlockSpec((1, D), lambda i: (0, 0)),
            pl.BlockSpec((1, D), lambda i: (0, 0)),
            pl.BlockSpec(memory_space=pl.ANY),
        ],
        out_specs=pl.BlockSpec((tc, D), lambda i: (i, 0)),
        out_shape=jax.ShapeDtypeStruct((T, D), F32),
        scratch_shapes=[pltpu.VMEM((2, TOP_K, tc, D // 2), jnp.uint32),
                        pltpu.VMEM((tc, V7X_LANES), F32),
                        pltpu.SemaphoreType.DMA((2,))],
        compiler_params=_cparams(1),
        name="moe_combine",
    )(pos, pos, y2, wts, ln2_g.reshape(1, D), ln2_b.reshape(1, D), ln_g.reshape(1, D), ln_b.reshape(1, D), yg)


def _tile_maps(counts, *, tm, n_tiles):
    counts = counts.astype(jnp.int32)
    tiles_per = (counts + tm - 1) // tm
    tile_end = jnp.cumsum(tiles_per)
    offsets = (tile_end - tiles_per) * tm
    w = jnp.arange(n_tiles, dtype=jnp.int32)
    te = jnp.sum((w[:, None] >= tile_end[None, :]).astype(jnp.int32), axis=1)
    valid = (w < tile_end[-1]).astype(jnp.int32)
    last_e = jnp.max(jnp.where(tiles_per > 0, jnp.arange(counts.shape[0], dtype=jnp.int32), 0))
    te = jnp.where(valid > 0, te, last_e).astype(jnp.int32)
    tr = jnp.minimum(w, tile_end[-1] - 1).astype(jnp.int32)
    return offsets, (te, valid, tr)


def kernel(x, mem, positions, w_in, b_gate, attn_sinks, conv_w, conv_b, w_lru_a, b_lru_a, w_lru_i,
           b_lru_i, lru_lambda, w_branch_attn, w_branch_lru, w_mix_out, ln1_g, ln1_b, w_mem_q, w_mem_kv,
           w_mem_o, ln2_g, ln2_b, w_router, b_router, w_gate_up, b_gate_up, w_down, b_down, ln3_g, ln3_b):
    B, S, D = x.shape
    T = B * S
    l = 0
    n_q_heads = D // HEAD_DIM
    n_kv_heads = n_q_heads // KV_GROUP
    q_width = n_q_heads * HEAD_DIM
    kv_width = n_kv_heads * HEAD_DIM
    lru_width = w_lru_a.shape[1] * LRU_BLOCK_DIM
    col_xl = q_width + 2 * kv_width
    col_yl = col_xl + lru_width
    col_ga = col_yl + lru_width
    col_gl = col_ga + D

    xf = x.reshape(T, D)
    xb = xf.astype(BF16)

    h = matmul(xb, w_in[l].astype(BF16), tm=1024, tn=512, out_dtype=BF16, name="in_proj")

    half = ROPE_DIM // 2
    inv_freq = 1.0 / (ROPE_THETA ** (np.arange(0, ROPE_DIM, 2, dtype=np.float32) / ROPE_DIM))
    freq_row = np.zeros((1, HEAD_DIM), np.float32)
    freq_row[0, :half] = inv_freq
    freq_row[0, half:ROPE_DIM] = inv_freq
    pos_col = positions.astype(F32).reshape(T, 1)
    qk = rope(h, pos_col, jnp.asarray(freq_row), n_q_heads=n_q_heads, n_k_heads=n_kv_heads,
              scale=HEAD_DIM ** -0.5, tm=256)
    sink_rows = jnp.broadcast_to(
        jnp.repeat(attn_sinks[l].astype(F32), ATTN_BLOCK).reshape(n_kv_heads, KV_GROUP * ATTN_BLOCK, 1),
        (n_kv_heads, KV_GROUP * ATTN_BLOCK, V7X_LANES))
    o_attn = sliding_window_attention(
        qk, h, jnp.asarray(_swa_bias()), sink_rows, batch=B, seq=S, n_kv_heads=n_kv_heads,
        k_col0=q_width // kv_width, v_col0=(q_width + kv_width) // kv_width)

    o_lru = rglru(h, conv_w[l], conv_b[l], w_lru_a[l], b_lru_a[l], w_lru_i[l], b_lru_i[l], lru_lambda[l],
                  batch=B, seq=S, x_col0=col_xl // LRU_BLOCK_DIM, y_col0=col_yl // LRU_BLOCK_DIM,
                  tt=LRU_TIME_TILE)

    tn_merge = 256
    mixed = gated_merge(o_attn, o_lru, w_branch_attn[l].astype(BF16), w_branch_lru[l].astype(BF16), h,
                        b_gate[l], ga_col0=col_ga // tn_merge, gl_col0=col_gl // tn_merge, tm=1024, tn=tn_merge)
    y1 = matmul(mixed, w_mix_out[l].astype(BF16), tm=1024, tn=512, out_dtype=F32, res=xf,
                alpha=DEEPNORM_ALPHA, name="mix_out")

    M = mem.shape[1]
    kv = matmul(mem.reshape(B * M, D).astype(BF16), w_mem_kv[l].astype(BF16), tm=B * M, tn=512,
                out_dtype=BF16, name="mem_kv")
    y2 = memory_cross_attention(y1, ln1_g[l], ln1_b[l], w_mem_q[l].astype(BF16), kv,
                                w_mem_o[l].astype(BF16), seq=S, mem_len=M, tm=256)

    x2p, idx, rank, wts, cnt = router(y2, ln2_g[l], ln2_b[l], w_router[l].T, b_router[l], tm=256)
    n_tiles = (T * TOP_K) // MOE_ROW_TILE + N_EXPERTS
    offsets, tile_maps = _tile_maps(cnt[:, 0], tm=MOE_ROW_TILE, n_tiles=n_tiles)
    offsets_f = jnp.broadcast_to(offsets.astype(F32)[:, None], (N_EXPERTS, V7X_LANES))
    pos = grouped_positions(idx, rank, offsets_f, tm=2048)

    rows = n_tiles * MOE_ROW_TILE
    xg = dispatch(x2p, pos, rows=rows, td=DISPATCH_TILE)
    hg = moe_up(xg, w_gate_up[l], b_gate_up[l], tile_maps, tm=MOE_ROW_TILE, tf=MOE_F_TILE)
    yg = moe_down(hg, w_down[l], b_down[l], tile_maps, tm=MOE_ROW_TILE, tn=MOE_N_TILE)
    out = combine(y2, wts, pos, yg, ln2_g[l], ln2_b[l], ln3_g[l], ln3_b[l], tc=DISPATCH_TILE)
    return out.reshape(B, S, D)
```

```python
import functools

import numpy as np
import jax
import jax.numpy as jnp
from jax import lax
from jax.experimental import pallas as pl
from jax.experimental.pallas import tpu as pltpu

F32 = jnp.float32
BF16 = jnp.bfloat16

HEAD_DIM = 128
KV_GROUP = 4
ATTN_BLOCK = 128
ROPE_DIM = HEAD_DIM // 4
ROPE_THETA = 500000.0
LRU_BLOCK_DIM = 256
CONV_WIDTH = 4
LRU_C = 8.0
MEM_HEADS = 4
MEM_HEAD_DIM = 128
N_EXPERTS = 32
TOP_K = 4
SWIGLU_LIMIT = 7.0
SWIGLU_ALPHA = 1.702
LN_EPS = 1e-5
DEPTH = 1
DEEPNORM_ALPHA = (2.0 * DEPTH) ** 0.25

V7X_LANES = 128
V7X_SUBLANES = 8
V7X_VMEM_LIMIT_BYTES = 56 * 1024 * 1024

MOE_ROW_TILE = 512
MOE_F_TILE = 512
MOE_N_TILE = 1024
LRU_TIME_TILE = 512
DISPATCH_TILE = 128


def _cparams(n_axes):
    return pltpu.CompilerParams(dimension_semantics=("arbitrary",) * n_axes,
                                vmem_limit_bytes=V7X_VMEM_LIMIT_BYTES)


def _layer_norm_rows(y, g, b):
    mu = jnp.mean(y, axis=-1, keepdims=True)
    d = y - mu
    var = jnp.mean(d * d, axis=-1, keepdims=True)
    return d * lax.rsqrt(var + LN_EPS) * g + b


def _pack_bf16_pairs(x):
    n = x.shape[1] // 2
    u = pltpu.bitcast(x, jnp.uint32)
    r = (u + jnp.uint32(0x7FFF) + ((u >> 16) & jnp.uint32(1))) >> 16
    return (r[:, n:] << 16) | r[:, :n]


def _unpack_bf16_pairs(w):
    lo = pltpu.bitcast(w << 16, F32).astype(BF16)
    hi = pltpu.bitcast(w & jnp.uint32(0xFFFF0000), F32).astype(BF16)
    return lo, hi


def _mm_kernel(a_ref, b_ref, o_ref):
    o_ref[...] = jnp.dot(a_ref[...].astype(BF16), b_ref[...].astype(BF16),
                         preferred_element_type=F32).astype(o_ref.dtype)


def _mm_cast_kernel(a_ref, b_ref, o_ref, abf_ref):
    @pl.when(pl.program_id(1) == 0)
    def _():
        abf_ref[...] = a_ref[...].astype(BF16)

    o_ref[...] = jnp.dot(abf_ref[...], b_ref[...], preferred_element_type=F32).astype(o_ref.dtype)


def _mm_res_kernel(a_ref, b_ref, r_ref, o_ref, *, alpha):
    acc = jnp.dot(a_ref[...].astype(BF16), b_ref[...].astype(BF16), preferred_element_type=F32)
    o_ref[...] = (alpha * r_ref[...] + acc).astype(o_ref.dtype)


def matmul(a, b, *, tm, tn, out_dtype, res=None, alpha=1.0, name="matmul"):
    M, K = a.shape
    N = b.shape[1]
    assert M % tm == 0 and N % tn == 0
    in_specs = [pl.BlockSpec((tm, K), lambda i, j: (i, 0)),
                pl.BlockSpec((K, tn), lambda i, j: (0, j))]
    args = [a, b]
    scratch = []
    if res is None and a.dtype == F32:
        body = _mm_cast_kernel
        scratch = [pltpu.VMEM((tm, K), BF16)]
    elif res is None:
        body = _mm_kernel
    else:
        body = functools.partial(_mm_res_kernel, alpha=alpha)
        in_specs.append(pl.BlockSpec((tm, tn), lambda i, j: (i, j)))
        args.append(res)
    return pl.pallas_call(
        body,
        grid=(M // tm, N // tn),
        in_specs=in_specs,
        out_specs=pl.BlockSpec((tm, tn), lambda i, j: (i, j)),
        out_shape=jax.ShapeDtypeStruct((M, N), out_dtype),
        scratch_shapes=scratch,
        compiler_params=_cparams(2),
        name=name,
    )(*args)


def _rope_kernel(pos_ref, freq_ref, h_ref, o_ref, *, n_q_heads, scale):
    ang = pos_ref[...] * freq_ref[...]
    lane = lax.broadcasted_iota(jnp.int32, ang.shape, 1)
    cos = jnp.cos(ang)
    sin = jnp.sin(ang)
    half = ROPE_DIM // 2
    c_mul = jnp.where(lane < ROPE_DIM, cos, 1.0)
    s_lo = jnp.where(lane < half, -sin, 0.0)
    s_hi = jnp.where((lane >= half) & (lane < ROPE_DIM), sin, 0.0)
    n_heads = h_ref.shape[1] // HEAD_DIM
    for hd in range(n_heads):
        t = h_ref[:, hd * HEAD_DIM:(hd + 1) * HEAD_DIM].astype(F32)
        up = pltpu.roll(t, HEAD_DIM - half, axis=1)
        dn = pltpu.roll(t, half, axis=1)
        r = t * c_mul + up * s_lo + dn * s_hi
        if hd < n_q_heads:
            r = r * scale
        o_ref[:, hd * HEAD_DIM:(hd + 1) * HEAD_DIM] = r.astype(o_ref.dtype)


def rope(h, pos_col, freq_row, *, n_q_heads, n_k_heads, scale, tm):
    T = h.shape[0]
    width = (n_q_heads + n_k_heads) * HEAD_DIM
    return pl.pallas_call(
        functools.partial(_rope_kernel, n_q_heads=n_q_heads, scale=scale),
        grid=(T // tm,),
        in_specs=[pl.BlockSpec((tm, 1), lambda i: (i, 0)),
                  pl.BlockSpec((1, HEAD_DIM), lambda i: (0, 0)),
                  pl.BlockSpec((tm, width), lambda i: (i, 0))],
        out_specs=pl.BlockSpec((tm, width), lambda i: (i, 0)),
        out_shape=jax.ShapeDtypeStruct((T, width), BF16),
        compiler_params=_cparams(1),
        name="rope",
    )(pos_col, freq_row, h)


def _swa_kernel(bias_ref, sink_ref, q_ref, kc_ref, kp_ref, vc_ref, vp_ref, o_ref):
    bias = bias_ref[0]
    qw = KV_GROUP * HEAD_DIM
    n_kv_heads = kc_ref.shape[1] // HEAD_DIM
    for g in range(n_kv_heads):
        kv_cols = slice(g * HEAD_DIM, (g + 1) * HEAD_DIM)
        q4 = jnp.concatenate(
            [q_ref[:, g * qw + i * HEAD_DIM:g * qw + (i + 1) * HEAD_DIM] for i in range(KV_GROUP)], axis=0)
        k = jnp.concatenate([kp_ref[:, kv_cols], kc_ref[:, kv_cols]], axis=0)
        v = jnp.concatenate([vp_ref[:, kv_cols], vc_ref[:, kv_cols]], axis=0)
        s = lax.dot_general(q4, k, (((1,), (1,)), ((), ())), preferred_element_type=F32) + bias
        sink = sink_ref[g][:, :1]
        m = jnp.maximum(jnp.max(s, axis=-1, keepdims=True), sink)
        p = jnp.exp(s - m)
        denom = jnp.sum(p, axis=-1, keepdims=True) + jnp.exp(sink - m)
        o = jnp.dot(p.astype(BF16), v, preferred_element_type=F32) / denom
        for i in range(KV_GROUP):
            o_ref[:, g * qw + i * HEAD_DIM:g * qw + (i + 1) * HEAD_DIM] = (
                o[i * ATTN_BLOCK:(i + 1) * ATTN_BLOCK].astype(o_ref.dtype))


def sliding_window_attention(qk, h, bias, sink_rows, *, batch, seq, n_kv_heads, k_col0, v_col0):
    nb = seq // ATTN_BLOCK
    T = batch * seq
    q_width = n_kv_heads * KV_GROUP * HEAD_DIM
    kv_width = n_kv_heads * HEAD_DIM

    def cur(b, n):
        return b * nb + n

    def prev(b, n):
        return b * nb + jnp.maximum(n - 1, 0)

    return pl.pallas_call(
        _swa_kernel,
        grid=(batch, nb),
        in_specs=[
            pl.BlockSpec((1, KV_GROUP * ATTN_BLOCK, 2 * ATTN_BLOCK), lambda b, n: (jnp.minimum(n, 1), 0, 0)),
            pl.BlockSpec((n_kv_heads, KV_GROUP * ATTN_BLOCK, V7X_LANES), lambda b, n: (0, 0, 0)),
            pl.BlockSpec((ATTN_BLOCK, q_width), lambda b, n: (cur(b, n), 0)),
            pl.BlockSpec((ATTN_BLOCK, kv_width), lambda b, n: (cur(b, n), k_col0)),
            pl.BlockSpec((ATTN_BLOCK, kv_width), lambda b, n: (prev(b, n), k_col0)),
            pl.BlockSpec((ATTN_BLOCK, kv_width), lambda b, n: (cur(b, n), v_col0)),
            pl.BlockSpec((ATTN_BLOCK, kv_width), lambda b, n: (prev(b, n), v_col0)),
        ],
        out_specs=pl.BlockSpec((ATTN_BLOCK, q_width), lambda b, n: (cur(b, n), 0)),
        out_shape=jax.ShapeDtypeStruct((T, q_width), BF16),
        compiler_params=_cparams(2),
        name="swa",
    )(bias, sink_rows, qk, qk, qk, h, h)


def _swa_bias():
    qi = np.arange(ATTN_BLOCK)[:, None]
    si = np.arange(2 * ATTN_BLOCK)[None, :]
    delta = ATTN_BLOCK + qi - si
    band = (delta >= 0) & (delta < ATTN_BLOCK)
    first = band & (si >= ATTN_BLOCK)
    m = np.stack([first, band]).astype(np.float32)
    bias = np.where(m > 0, 0.0, -np.inf).astype(np.float32)
    return np.tile(bias, (1, KV_GROUP, 1))


def _gelu_tanh(y):
    return 0.5 * y * (1.0 + jnp.tanh(0.7978845608028654 * (y + 0.044715 * (y * y * y))))


def _lru_kernel(x_ref, y_ref, cw_ref, cb_ref, wa_ref, ba_ref, wi_ref, bi_ref, lam_ref, o_ref,
                tail_ref, carry_ref):
    tt = x_ref.shape[0]

    @pl.when(pl.program_id(2) == 0)
    def _():
        tail_ref[...] = jnp.zeros_like(tail_ref)
        carry_ref[...] = jnp.zeros_like(carry_ref)

    x = x_ref[...].astype(F32)
    ext = jnp.concatenate([tail_ref[...], x], axis=0)
    tail_ref[...] = x[tt - V7X_SUBLANES:, :]
    cw = cw_ref[...]
    xc = cb_ref[...] + cw[CONV_WIDTH - 1:CONV_WIDTH, :] * x
    for j in range(1, CONV_WIDTH):
        sh = pltpu.roll(ext, j, axis=0)[V7X_SUBLANES:, :]
        xc = xc + cw[CONV_WIDTH - 1 - j:CONV_WIDTH - j, :] * sh
    xcb = xc.astype(BF16)
    r = jax.nn.sigmoid(jnp.dot(xcb, wa_ref[0].astype(BF16), preferred_element_type=F32) + ba_ref[0])
    gi = jax.nn.sigmoid(jnp.dot(xcb, wi_ref[0].astype(BF16), preferred_element_type=F32) + bi_ref[0])
    z = -lam_ref[...]
    softplus = jnp.maximum(z, 0.0) + jnp.log(1.0 + jnp.exp(-jnp.abs(z)))
    log_a = (-LRU_C) * r * softplus
    a = jnp.exp(log_a)
    b = jnp.sqrt(1.0 - a * a) * (gi * xc)

    row = lax.broadcasted_iota(jnp.int32, a.shape, 0)
    d = 1
    while d < tt:
        if d < V7X_SUBLANES:
            a_sh = jnp.where(row >= d, pltpu.roll(a, d, axis=0), 1.0)
            b_sh = jnp.where(row >= d, pltpu.roll(b, d, axis=0), 0.0)
            b = a * b_sh + b
            a = a * a_sh
        else:
            b = jnp.concatenate([b[:d], a[d:] * b[:-d] + b[d:]], axis=0)
            a = jnp.concatenate([a[:d], a[d:] * a[:-d]], axis=0)
        d *= 2
    h_prev = carry_ref[V7X_SUBLANES - 1:V7X_SUBLANES, :]
    h = a * h_prev + b
    carry_ref[...] = h[tt - V7X_SUBLANES:, :]
    o_ref[...] = (h * _gelu_tanh(y_ref[...].astype(F32))).astype(o_ref.dtype)


def rglru(h, conv_w, conv_b, w_a, b_a, w_i, b_i, lam, *, batch, seq, x_col0, y_col0, tt):
    nblk = w_a.shape[0]
    C = nblk * LRU_BLOCK_DIM
    nt = seq // tt
    T = batch * seq
    cb = LRU_BLOCK_DIM
    vec = lambda b, c, t: (0, c)
    blk3 = lambda b, c, t: (c, 0, 0)
    return pl.pallas_call(
        _lru_kernel,
        grid=(batch, nblk, nt),
        in_specs=[
            pl.BlockSpec((tt, cb), lambda b, c, t: (b * nt + t, x_col0 + c)),
            pl.BlockSpec((tt, cb), lambda b, c, t: (b * nt + t, y_col0 + c)),
            pl.BlockSpec((CONV_WIDTH, cb), vec),
            pl.BlockSpec((1, cb), vec),
            pl.BlockSpec((1, cb, cb), blk3),
            pl.BlockSpec((1, 1, cb), blk3),
            pl.BlockSpec((1, cb, cb), blk3),
            pl.BlockSpec((1, 1, cb), blk3),
            pl.BlockSpec((1, cb), vec),
        ],
        out_specs=pl.BlockSpec((tt, cb), lambda b, c, t: (b * nt + t, c)),
        out_shape=jax.ShapeDtypeStruct((T, C), BF16),
        scratch_shapes=[pltpu.VMEM((V7X_SUBLANES, cb), F32), pltpu.VMEM((V7X_SUBLANES, cb), F32)],
        compiler_params=_cparams(3),
        name="rglru",
    )(h, h, conv_w, conv_b.reshape(1, C), w_a, b_a.reshape(nblk, 1, cb), w_i,
      b_i.reshape(nblk, 1, cb), lam.reshape(1, C))


def _merge_kernel(oa_ref, ol_ref, wa_ref, wl_ref, ga_ref, gl_ref, bg_ref, o_ref):
    pa = jnp.dot(oa_ref[...], wa_ref[...].astype(BF16), preferred_element_type=F32)
    pb = jnp.dot(ol_ref[...], wl_ref[...].astype(BF16), preferred_element_type=F32)
    bg = bg_ref[...]
    g_a = jax.nn.sigmoid(ga_ref[...].astype(F32) + bg[0:1, :])
    g_l = jax.nn.sigmoid(gl_ref[...].astype(F32) + bg[1:2, :])
    o_ref[...] = (g_a * pa + g_l * pb).astype(o_ref.dtype)


def gated_merge(oa, ol, wa, wl, h, b_gate, *, ga_col0, gl_col0, tm, tn):
    T, K = oa.shape
    N = wa.shape[1]
    return pl.pallas_call(
        _merge_kernel,
        grid=(T // tm, N // tn),
        in_specs=[
            pl.BlockSpec((tm, K), lambda i, j: (i, 0)),
            pl.BlockSpec((tm, K), lambda i, j: (i, 0)),
            pl.BlockSpec((K, tn), lambda i, j: (0, j)),
            pl.BlockSpec((K, tn), lambda i, j: (0, j)),
            pl.BlockSpec((tm, tn), lambda i, j: (i, ga_col0 + j)),
            pl.BlockSpec((tm, tn), lambda i, j: (i, gl_col0 + j)),
            pl.BlockSpec((2, tn), lambda i, j: (0, j)),
        ],
        out_specs=pl.BlockSpec((tm, tn), lambda i, j: (i, j)),
        out_shape=jax.ShapeDtypeStruct((T, N), BF16),
        compiler_params=_cparams(2),
        name="gated_merge",
    )(oa, ol, wa, wl, h, h, b_gate)


def _xattn_kernel(y_ref, g_ref, b_ref, wq_ref, kv_ref, wo_ref, o_ref, *, scale, alpha):
    x1 = _layer_norm_rows(y_ref[...], g_ref[...], b_ref[...])
    q = jnp.dot(x1.astype(BF16), wq_ref[...], preferred_element_type=F32) * scale
    qb = q.astype(BF16)
    width = MEM_HEADS * MEM_HEAD_DIM
    outs = []
    for hd in range(MEM_HEADS):
        sl = slice(hd * MEM_HEAD_DIM, (hd + 1) * MEM_HEAD_DIM)
        k = kv_ref[:, sl]
        v = kv_ref[:, width + hd * MEM_HEAD_DIM:width + (hd + 1) * MEM_HEAD_DIM]
        s = lax.dot_general(qb[:, sl], k, (((1,), (1,)), ((), ())), preferred_element_type=F32)
        m = jnp.max(s, axis=-1, keepdims=True)
        p = jnp.exp(s - m)
        den = jnp.sum(p, axis=-1, keepdims=True)
        outs.append(jnp.dot(p.astype(BF16), v, preferred_element_type=F32) / den)
    o = jnp.concatenate(outs, axis=1).astype(BF16)
    xa = jnp.dot(o, wo_ref[...], preferred_element_type=F32)
    o_ref[...] = alpha * x1 + xa


def memory_cross_attention(y1, ln_g, ln_b, wq, kv, wo, *, seq, mem_len, tm):
    T, D = y1.shape
    per_batch = seq // tm
    width = MEM_HEADS * MEM_HEAD_DIM
    return pl.pallas_call(
        functools.partial(_xattn_kernel, scale=MEM_HEAD_DIM ** -0.5, alpha=DEEPNORM_ALPHA),
        grid=(T // tm,),
        in_specs=[
            pl.BlockSpec((tm, D), lambda i: (i, 0)),
            pl.BlockSpec((1, D), lambda i: (0, 0)),
            pl.BlockSpec((1, D), lambda i: (0, 0)),
            pl.BlockSpec((D, width), lambda i: (0, 0)),
            pl.BlockSpec((mem_len, 2 * width), lambda i: (i // per_batch, 0)),
            pl.BlockSpec((width, D), lambda i: (0, 0)),
        ],
        out_specs=pl.BlockSpec((tm, D), lambda i: (i, 0)),
        out_shape=jax.ShapeDtypeStruct((T, D), F32),
        compiler_params=_cparams(1),
        name="mem_xattn",
    )(y1, ln_g.reshape(1, D), ln_b.reshape(1, D), wq, kv, wo)


def _router_kernel(y_ref, g_ref, b_ref, wr_ref, br_ref, x2_ref, idx_ref, rank_ref, wts_ref, cnt_ref,
                   carry_ref):
    tm = y_ref.shape[0]

    @pl.when(pl.program_id(0) == 0)
    def _():
        carry_ref[...] = jnp.zeros_like(carry_ref)

    x2 = _layer_norm_rows(y_ref[...], g_ref[...], b_ref[...])
    x2_ref[...] = _pack_bf16_pairs(x2)
    logits = lax.dot_general(wr_ref[...], x2, (((1,), (1,)), ((), ())),
                             precision=lax.Precision.HIGHEST, preferred_element_type=F32)
    logits = logits + br_ref[...][:, :1]
    eidx = lax.broadcasted_iota(jnp.int32, logits.shape, 0)
    work = logits
    vals, idxs, hots = [], [], []
    for _ in range(TOP_K):
        m = jnp.max(work, axis=0, keepdims=True)
        idx = jnp.min(jnp.where(work == m, eidx, N_EXPERTS), axis=0, keepdims=True)
        hot = eidx == idx
        vals.append(m)
        idxs.append(idx)
        hots.append(hot)
        work = jnp.where(hot, -jnp.inf, work)
    exps = [jnp.exp(v - vals[0]) for v in vals]
    den = exps[0] + exps[1] + exps[2] + exps[3]
    zero_row = jnp.zeros_like(den)
    wts_ref[...] = jnp.concatenate([e / den for e in exps] + [zero_row] * (V7X_SUBLANES - TOP_K), axis=0)

    sel = jnp.zeros(logits.shape, F32)
    for hot in hots:
        sel = sel + jnp.where(hot, 1.0, 0.0)
    r_i = lax.broadcasted_iota(jnp.int32, (tm, tm), 0)
    c_i = lax.broadcasted_iota(jnp.int32, (tm, tm), 1)
    upper = jnp.where(r_i <= c_i, 1.0, 0.0).astype(BF16)
    incl = jnp.dot(sel.astype(BF16), upper, preferred_element_type=F32)
    carry = carry_ref[...][:, :1]
    excl = carry + incl - sel
    ranks = [jnp.sum(jnp.where(hot, excl, 0.0), axis=0, keepdims=True) for hot in hots]
    rank_ref[...] = jnp.concatenate(ranks, axis=0).astype(jnp.int32)
    idx_ref[...] = jnp.concatenate(idxs, axis=0)
    new_carry = carry + incl[:, tm - 1:tm]
    carry_ref[...] = jnp.broadcast_to(new_carry, carry_ref.shape)
    cnt_ref[...] = jnp.broadcast_to(new_carry, cnt_ref.shape)


def router(y2, ln_g, ln_b, w_router_t, b_router, *, tm):
    T, D = y2.shape
    E = w_router_t.shape[0]
    br = jnp.broadcast_to(b_router.reshape(E, 1), (E, V7X_LANES))
    return pl.pallas_call(
        _router_kernel,
        grid=(T // tm,),
        in_specs=[
            pl.BlockSpec((tm, D), lambda i: (i, 0)),
            pl.BlockSpec((1, D), lambda i: (0, 0)),
            pl.BlockSpec((1, D), lambda i: (0, 0)),
            pl.BlockSpec((E, D), lambda i: (0, 0)),
            pl.BlockSpec((E, V7X_LANES), lambda i: (0, 0)),
        ],
        out_specs=[
            pl.BlockSpec((tm, D // 2), lambda i: (i, 0)),
            pl.BlockSpec((TOP_K, tm), lambda i: (0, i)),
            pl.BlockSpec((TOP_K, tm), lambda i: (0, i)),
            pl.BlockSpec((V7X_SUBLANES, tm), lambda i: (0, i)),
            pl.BlockSpec((E, V7X_LANES), lambda i: (0, 0)),
        ],
        out_shape=[
            jax.ShapeDtypeStruct((T, D // 2), jnp.uint32),
            jax.ShapeDtypeStruct((TOP_K, T), jnp.int32),
            jax.ShapeDtypeStruct((TOP_K, T), jnp.int32),
            jax.ShapeDtypeStruct((V7X_SUBLANES, T), F32),
            jax.ShapeDtypeStruct((E, V7X_LANES), F32),
        ],
        scratch_shapes=[pltpu.VMEM((E, V7X_LANES), F32)],
        compiler_params=_cparams(1),
        name="router",
    )(y2, ln_g.reshape(1, D), ln_b.reshape(1, D), w_router_t, br)


def _pos_kernel(idx_ref, rank_ref, off_ref, pos_ref):
    off = off_ref[...][:, :1]
    shape = (off_ref.shape[0], idx_ref.shape[1])
    eidx = lax.broadcasted_iota(jnp.int32, shape, 0)
    rows = []
    for k in range(TOP_K):
        hot = eidx == idx_ref[k:k + 1, :]
        rows.append(jnp.sum(jnp.where(hot, off, 0.0), axis=0, keepdims=True))
    pos_ref[...] = jnp.concatenate(rows, axis=0).astype(jnp.int32) + rank_ref[...]


def grouped_positions(idx, rank, offsets_f, *, tm):
    T = idx.shape[1]
    E = offsets_f.shape[0]
    return pl.pallas_call(
        _pos_kernel,
        grid=(T // tm,),
        in_specs=[pl.BlockSpec((TOP_K, tm), lambda i: (0, i)),
                  pl.BlockSpec((TOP_K, tm), lambda i: (0, i)),
                  pl.BlockSpec((E, V7X_LANES), lambda i: (0, 0))],
        out_specs=pl.BlockSpec((TOP_K, tm), lambda i: (0, i)),
        out_shape=jax.ShapeDtypeStruct((TOP_K, T), jnp.int32),
        compiler_params=_cparams(1),
        name="grouped_pos",
    )(idx, rank, offsets_f)


def _dispatch_kernel(pos_ref, pad_ref, x_ref, xg_ref, zero_ref, sem, zero_sem):
    td = x_ref.shape[0]
    tm = zero_ref.shape[0]

    @pl.when(pl.program_id(0) == 0)
    def _():
        zero_ref[...] = jnp.zeros_like(zero_ref)

        def zero_copy(w):
            return pltpu.make_async_copy(zero_ref, xg_ref.at[pl.ds(pl.multiple_of(w * tm, tm), tm)], zero_sem)

        def start(w, c):
            @pl.when(pad_ref[w] > 0)
            def _():
                zero_copy(w).start()
            return c

        def wait(w, c):
            @pl.when(pad_ref[w] > 0)
            def _():
                zero_copy(w).wait()
            return c

        lax.fori_loop(0, pad_ref.shape[0], start, 0)
        lax.fori_loop(0, pad_ref.shape[0], wait, 0)

    def row_copy(t, p):
        return pltpu.make_async_copy(x_ref.at[pl.ds(t, 1)], xg_ref.at[pl.ds(p, 1)], sem)

    def issue(t, c):
        for k in range(TOP_K):
            row_copy(t, pos_ref[k, t]).start()
        return c

    lax.fori_loop(0, td, issue, 0)

    for k in range(TOP_K):
        pltpu.make_async_copy(x_ref, xg_ref.at[pl.ds(0, td)], sem).wait()


def dispatch(x2, pos, pad_tiles, *, tm, td):
    T, D = x2.shape
    rows = pad_tiles.shape[0] * tm
    return pl.pallas_call(
        _dispatch_kernel,
        grid=(T // td,),
        in_specs=[pl.BlockSpec((TOP_K, td), lambda i: (0, i), memory_space=pltpu.SMEM),
                  pl.BlockSpec(memory_space=pltpu.SMEM),
                  pl.BlockSpec((td, D), lambda i: (i, 0))],
        out_specs=pl.BlockSpec(memory_space=pl.ANY),
        out_shape=jax.ShapeDtypeStruct((rows, D), x2.dtype),
        scratch_shapes=[pltpu.VMEM((tm, D), x2.dtype), pltpu.SemaphoreType.DMA(()), pltpu.SemaphoreType.DMA(())],
        compiler_params=_cparams(1),
        name="moe_dispatch",
    )(pos, pad_tiles, x2)


def _expert_weight_stream(w_hbm, wbuf, sem, maps, *, width, hi_col0):
    te_ref, tv_ref, _, first_ref, next_ref, slot_ref = maps
    j = pl.program_id(0)
    w = pl.program_id(1)
    slot = slot_ref[w]
    valid = tv_ref[w] > 0

    def slab_copies(e, s):
        return [pltpu.make_async_copy(
            w_hbm.at[e, :, pl.ds(pl.multiple_of(j * width + c * hi_col0, V7X_LANES), width)],
            wbuf.at[s, c], sem.at[s]) for c in range(2)]

    @pl.when(w == 0)
    def _():
        for cp in slab_copies(te_ref[0], slot):
            cp.start()

    @pl.when(valid & (first_ref[w] > 0))
    def _():
        for cp in slab_copies(te_ref[w], slot):
            cp.wait()

        @pl.when(next_ref[w] >= 0)
        def _():
            for cp in slab_copies(next_ref[w], 1 - slot):
                cp.start()

    return valid, slot


def _moe_up_kernel(te_ref, tv_ref, tr_ref, first_ref, next_ref, slot_ref, x_ref, w_hbm, bg_ref, bu_ref, o_ref,
                   wbuf, sem):
    maps = (te_ref, tv_ref, tr_ref, first_ref, next_ref, slot_ref)
    tf = o_ref.shape[1]
    valid, slot = _expert_weight_stream(w_hbm, wbuf, sem, maps, width=tf, hi_col0=w_hbm.shape[2] // 2)

    @pl.when(valid)
    def _():
        x_lo, x_hi = _unpack_bf16_pairs(x_ref[...])
        n = x_lo.shape[1]

        def proj(c, b_ref):
            return (jnp.dot(x_lo, wbuf[slot, c, :n, :].astype(BF16), preferred_element_type=F32)
                    + jnp.dot(x_hi, wbuf[slot, c, n:, :].astype(BF16), preferred_element_type=F32) + b_ref[0])

        gate = proj(0, bg_ref)
        up = proj(1, bu_ref)
        gate = jnp.minimum(gate, SWIGLU_LIMIT)
        up = jnp.clip(up, -SWIGLU_LIMIT, SWIGLU_LIMIT)
        glu = gate * jax.nn.sigmoid(SWIGLU_ALPHA * gate)
        o_ref[...] = ((up + 1.0) * glu).astype(o_ref.dtype)

    @pl.when(jnp.logical_not(valid))
    def _():
        o_ref[...] = jnp.zeros_like(o_ref)


def moe_up(xg, w_gate_up, b_gate_up, tile_maps, *, tm, tf):
    R, half_d = xg.shape
    E, D, F2 = w_gate_up.shape
    assert D == 2 * half_d
    F = F2 // 2
    nf = F // tf
    nt = R // tm
    grid_spec = pltpu.PrefetchScalarGridSpec(
        num_scalar_prefetch=len(tile_maps),
        grid=(nf, nt),
        in_specs=[
            pl.BlockSpec((tm, half_d), lambda j, w, te, tv, tr, *_: (tr[w], 0)),
            pl.BlockSpec(memory_space=pl.ANY),
            pl.BlockSpec((1, 1, tf), lambda j, w, te, tv, tr, *_: (te[w], 0, j)),
            pl.BlockSpec((1, 1, tf), lambda j, w, te, tv, tr, *_: (te[w], 0, nf + j)),
        ],
        out_specs=pl.BlockSpec((tm, tf), lambda j, w, *_: (w, j)),
        scratch_shapes=[pltpu.VMEM((2, 2, D, tf), w_gate_up.dtype), pltpu.SemaphoreType.DMA((2,))],
    )
    b3 = b_gate_up.reshape(E, 1, F2)
    return pl.pallas_call(
        _moe_up_kernel,
        grid_spec=grid_spec,
        out_shape=jax.ShapeDtypeStruct((R, F), BF16),
        compiler_params=_cparams(2),
        name="moe_up",
    )(*tile_maps, xg, w_gate_up, b3, b3)


def _moe_down_kernel(te_ref, tv_ref, tr_ref, first_ref, next_ref, slot_ref, h_ref, w_hbm, blo_ref, bhi_ref, o_ref,
                     wbuf, sem):
    maps = (te_ref, tv_ref, tr_ref, first_ref, next_ref, slot_ref)
    tn = o_ref.shape[1]
    valid, slot = _expert_weight_stream(w_hbm, wbuf, sem, maps, width=tn, hi_col0=w_hbm.shape[2] // 2)

    @pl.when(valid)
    def _():
        h = h_ref[...]
        lo = jnp.dot(h, wbuf[slot, 0].astype(BF16), preferred_element_type=F32) + blo_ref[0]
        hi = jnp.dot(h, wbuf[slot, 1].astype(BF16), preferred_element_type=F32) + bhi_ref[0]
        o_ref[...] = _pack_bf16_pairs(jnp.concatenate([lo, hi], axis=1))

    @pl.when(jnp.logical_not(valid))
    def _():
        o_ref[...] = jnp.zeros_like(o_ref)


def moe_down(hg, w_down, b_down, tile_maps, *, tm, tn):
    R, F = hg.shape
    E, _, D = w_down.shape
    half = D // 2
    nn = half // tn
    nt = R // tm
    grid_spec = pltpu.PrefetchScalarGridSpec(
        num_scalar_prefetch=len(tile_maps),
        grid=(nn, nt),
        in_specs=[
            pl.BlockSpec((tm, F), lambda j, w, te, tv, tr, *_: (tr[w], 0)),
            pl.BlockSpec(memory_space=pl.ANY),
            pl.BlockSpec((1, 1, tn), lambda j, w, te, tv, tr, *_: (te[w], 0, j)),
            pl.BlockSpec((1, 1, tn), lambda j, w, te, tv, tr, *_: (te[w], 0, nn + j)),
        ],
        out_specs=pl.BlockSpec((tm, tn), lambda j, w, *_: (w, j)),
        scratch_shapes=[pltpu.VMEM((2, 2, F, tn), w_down.dtype), pltpu.SemaphoreType.DMA((2,))],
    )
    b3 = b_down.reshape(E, 1, D)
    return pl.pallas_call(
        _moe_down_kernel,
        grid_spec=grid_spec,
        out_shape=jax.ShapeDtypeStruct((R, half), jnp.uint32),
        compiler_params=_cparams(2),
        name="moe_down",
    )(*tile_maps, hg, w_down, b3, b3)


COMBINE_ROW_CHUNK = 32


def _combine_kernel(pos_ref, pos_next_ref, y2_ref, w_ref, g2_ref, b2_ref, g_ref, b_ref, yg_ref, o_ref,
                    buf_ref, wcol_ref, sem, *, alpha):
    tc = y2_ref.shape[0]
    i = pl.program_id(0)
    slot = lax.rem(i, 2)

    def row_copy(s, k, t, p):
        return pltpu.make_async_copy(yg_ref.at[pl.ds(p, 1)], buf_ref.at[s, k, pl.ds(t, 1)], sem.at[s])

    def gather(p_ref, s):
        def issue(t, c):
            for k in range(TOP_K):
                row_copy(s, k, t, p_ref[k, t]).start()
            return c
        lax.fori_loop(0, tc, issue, 0)

    @pl.when(i == 0)
    def _():
        gather(pos_ref, 0)

    @pl.when(i + 1 < pl.num_programs(0))
    def _():
        gather(pos_next_ref, 1 - slot)

    for k in range(TOP_K):
        pltpu.make_async_copy(yg_ref.at[pl.ds(0, tc)], buf_ref.at[slot, k], sem.at[slot]).wait()

    wt = w_ref[...]
    wt = jnp.concatenate([wt, jnp.zeros((V7X_LANES - wt.shape[0], tc), F32)], axis=0)
    wcol_ref[...] = wt.T

    def chunk(c, carry):
        rows = pl.ds(pl.multiple_of(c * COMBINE_ROW_CHUNK, COMBINE_ROW_CHUNK), COMBINE_ROW_CHUNK)
        y = alpha * _layer_norm_rows(y2_ref[rows, :], g2_ref[...], b2_ref[...])
        w_cols = wcol_ref[rows, :]
        for k in range(TOP_K):
            word = buf_ref[slot, k, rows, :]
            lo = pltpu.bitcast(word << 16, F32)
            hi = pltpu.bitcast(word & jnp.uint32(0xFFFF0000), F32)
            y = y + w_cols[:, k:k + 1] * jnp.concatenate([lo, hi], axis=1)
        o_ref[rows, :] = _layer_norm_rows(y, g_ref[...], b_ref[...])
        return carry

    lax.fori_loop(0, tc // COMBINE_ROW_CHUNK, chunk, 0)


def combine(y2, wts, pos, yg, ln2_g, ln2_b, ln_g, ln_b, *, tc):
    T, D = y2.shape
    assert tc == V7X_LANES and yg.shape[1] * 2 == D
    n_steps = T // tc
    return pl.pallas_call(
        functools.partial(_combine_kernel, alpha=DEEPNORM_ALPHA),
        grid=(n_steps,),
        in_specs=[
            pl.BlockSpec((TOP_K, tc), lambda i: (0, i), memory_space=pltpu.SMEM),
            pl.BlockSpec((TOP_K, tc), lambda i: (0, jnp.minimum(i + 1, n_steps - 1)), memory_space=pltpu.SMEM),
            pl.BlockSpec((tc, D), lambda i: (i, 0)),
            pl.BlockSpec((V7X_SUBLANES, tc), lambda i: (0, i)),
            pl.BlockSpec((1, D), lambda i: (0, 0)),
            pl.BlockSpec((1, D), lambda i: (0, 0)),
            pl.BlockSpec((1, D), lambda i: (0, 0)),
            pl.BlockSpec((1, D), lambda i: (0, 0)),
            pl.BlockSpec(memory_space=pl.ANY),
        ],
        out_specs=pl.BlockSpec((tc, D), lambda i: (i, 0)),
        out_shape=jax.ShapeDtypeStruct((T, D), F32),
        scratch_shapes=[pltpu.VMEM((2, TOP_K, tc, D // 2), jnp.uint32),
                        pltpu.VMEM((tc, V7X_LANES), F32),
                        pltpu.SemaphoreType.DMA((2,))],
        compiler_params=_cparams(1),
        name="moe_combine",
    )(pos, pos, y2, wts, ln2_g.reshape(1, D), ln2_b.reshape(1, D), ln_g.reshape(1, D), ln_b.reshape(1, D), yg)


def _tile_maps(counts, *, tm, n_tiles):
    counts = counts.astype(jnp.int32)
    tiles_per = (counts + tm - 1) // tm
    tile_end = jnp.cumsum(tiles_per)
    offsets = (tile_end - tiles_per) * tm
    w = jnp.arange(n_tiles, dtype=jnp.int32)
    te = jnp.sum((w[:, None] >= tile_end[None, :]).astype(jnp.int32), axis=1)
    valid = (w < tile_end[-1]).astype(jnp.int32)
    last_e = jnp.max(jnp.where(tiles_per > 0, jnp.arange(counts.shape[0], dtype=jnp.int32), 0))
    te = jnp.where(valid > 0, te, last_e).astype(jnp.int32)
    tr = jnp.minimum(w, tile_end[-1] - 1).astype(jnp.int32)
    prev_e = jnp.concatenate([jnp.full((1,), -1, jnp.int32), te[:-1]])
    first = ((valid > 0) & (te != prev_e)).astype(jnp.int32)
    slot = ((jnp.cumsum(first) - 1) % 2).astype(jnp.int32)
    first_pos = jnp.where(first > 0, w, n_tiles)
    next_first = jnp.concatenate([lax.cummin(first_pos[::-1])[::-1][1:], jnp.full((1,), n_tiles, jnp.int32)])
    nxt = jnp.where(next_first < n_tiles, te[jnp.minimum(next_first, n_tiles - 1)], -1).astype(jnp.int32)
    next_is_first = jnp.concatenate([first[1:], jnp.ones((1,), jnp.int32)])
    pad = ((valid == 0) | (next_is_first > 0) | (w + 1 >= tile_end[-1])).astype(jnp.int32)
    return offsets, (te, valid, tr, first, nxt, slot), pad


def kernel(x, mem, positions, w_in, b_gate, attn_sinks, conv_w, conv_b, w_lru_a, b_lru_a, w_lru_i,
           b_lru_i, lru_lambda, w_branch_attn, w_branch_lru, w_mix_out, ln1_g, ln1_b, w_mem_q, w_mem_kv,
           w_mem_o, ln2_g, ln2_b, w_router, b_router, w_gate_up, b_gate_up, w_down, b_down, ln3_g, ln3_b):
    B, S, D = x.shape
    T = B * S
    l = 0
    n_q_heads = D // HEAD_DIM
    n_kv_heads = n_q_heads // KV_GROUP
    q_width = n_q_heads * HEAD_DIM
    kv_width = n_kv_heads * HEAD_DIM
    lru_width = w_lru_a.shape[1] * LRU_BLOCK_DIM
    col_xl = q_width + 2 * kv_width
    col_yl = col_xl + lru_width
    col_ga = col_yl + lru_width
    col_gl = col_ga + D

    xf = x.reshape(T, D)

    h = matmul(xf, w_in[l].astype(BF16), tm=1024, tn=512, out_dtype=BF16, name="in_proj")

    half = ROPE_DIM // 2
    inv_freq = 1.0 / (ROPE_THETA ** (np.arange(0, ROPE_DIM, 2, dtype=np.float32) / ROPE_DIM))
    freq_row = np.zeros((1, HEAD_DIM), np.float32)
    freq_row[0, :half] = inv_freq
    freq_row[0, half:ROPE_DIM] = inv_freq
    pos_col = positions.astype(F32).reshape(T, 1)
    qk = rope(h, pos_col, jnp.asarray(freq_row), n_q_heads=n_q_heads, n_k_heads=n_kv_heads,
              scale=HEAD_DIM ** -0.5, tm=256)
    sink_rows = jnp.broadcast_to(
        jnp.repeat(attn_sinks[l].astype(F32), ATTN_BLOCK).reshape(n_kv_heads, KV_GROUP * ATTN_BLOCK, 1),
        (n_kv_heads, KV_GROUP * ATTN_BLOCK, V7X_LANES))
    o_attn = sliding_window_attention(
        qk, h, jnp.asarray(_swa_bias()), sink_rows, batch=B, seq=S, n_kv_heads=n_kv_heads,
        k_col0=q_width // kv_width, v_col0=(q_width + kv_width) // kv_width)

    o_lru = rglru(h, conv_w[l], conv_b[l], w_lru_a[l], b_lru_a[l], w_lru_i[l], b_lru_i[l], lru_lambda[l],
                  batch=B, seq=S, x_col0=col_xl // LRU_BLOCK_DIM, y_col0=col_yl // LRU_BLOCK_DIM,
                  tt=LRU_TIME_TILE)

    tn_merge = 256
    mixed = gated_merge(o_attn, o_lru, w_branch_attn[l].astype(BF16), w_branch_lru[l].astype(BF16), h,
                        b_gate[l], ga_col0=col_ga // tn_merge, gl_col0=col_gl // tn_merge, tm=1024, tn=tn_merge)
    y1 = matmul(mixed, w_mix_out[l].astype(BF16), tm=1024, tn=512, out_dtype=F32, res=xf,
                alpha=DEEPNORM_ALPHA, name="mix_out")

    M = mem.shape[1]
    kv = matmul(mem.reshape(B * M, D).astype(BF16), w_mem_kv[l].astype(BF16), tm=B * M, tn=512,
                out_dtype=BF16, name="mem_kv")
    y2 = memory_cross_attention(y1, ln1_g[l], ln1_b[l], w_mem_q[l].astype(BF16), kv,
                                w_mem_o[l].astype(BF16), seq=S, mem_len=M, tm=256)

    x2p, idx, rank, wts, cnt = router(y2, ln2_g[l], ln2_b[l], w_router[l].T, b_router[l], tm=256)
    n_tiles = (T * TOP_K) // MOE_ROW_TILE + N_EXPERTS
    offsets, tile_maps, pad_tiles = _tile_maps(cnt[:, 0], tm=MOE_ROW_TILE, n_tiles=n_tiles)
    offsets_f = jnp.broadcast_to(offsets.astype(F32)[:, None], (N_EXPERTS, V7X_LANES))
    pos = grouped_positions(idx, rank, offsets_f, tm=2048)

    xg = dispatch(x2p, pos, pad_tiles, tm=MOE_ROW_TILE, td=DISPATCH_TILE)
    hg = moe_up(xg, w_gate_up[l], b_gate_up[l], tile_maps, tm=MOE_ROW_TILE, tf=MOE_F_TILE)
    yg = moe_down(hg, w_down[l], b_down[l], tile_maps, tm=MOE_ROW_TILE, tn=MOE_N_TILE)
    out = combine(y2, wts, pos, yg, ln2_g[l], ln2_b[l], ln3_g[l], ln3_b[l], tc=DISPATCH_TILE)
    return out.reshape(B, S, D)
```

```python
import functools

import numpy as np
import jax
import jax.numpy as jnp
from jax import lax
from jax.experimental import pallas as pl
from jax.experimental.pallas import tpu as pltpu

F32 = jnp.float32
BF16 = jnp.bfloat16

HEAD_DIM = 128
KV_GROUP = 4
ATTN_BLOCK = 128
ROPE_DIM = HEAD_DIM // 4
ROPE_THETA = 500000.0
LRU_BLOCK_DIM = 256
CONV_WIDTH = 4
LRU_C = 8.0
MEM_HEADS = 4
MEM_HEAD_DIM = 128
N_EXPERTS = 32
TOP_K = 4
SWIGLU_LIMIT = 7.0
SWIGLU_ALPHA = 1.702
LN_EPS = 1e-5
DEPTH = 1
DEEPNORM_ALPHA = (2.0 * DEPTH) ** 0.25

V7X_LANES = 128
V7X_SUBLANES = 8
V7X_VMEM_LIMIT_BYTES = 56 * 1024 * 1024

MOE_ROW_TILE = 512
MOE_F_TILE = 512
MOE_N_TILE = 1024
LRU_TIME_TILE = 512
DISPATCH_TILE = 256
COMBINE_TILE = 128


def _cparams(n_axes):
    return pltpu.CompilerParams(dimension_semantics=("arbitrary",) * n_axes,
                                vmem_limit_bytes=V7X_VMEM_LIMIT_BYTES)


def _layer_norm_rows(y, g, b):
    mu = jnp.mean(y, axis=-1, keepdims=True)
    d = y - mu
    var = jnp.mean(d * d, axis=-1, keepdims=True)
    return d * lax.rsqrt(var + LN_EPS) * g + b


def _pack_bf16_pairs(x):
    n = x.shape[1] // 2
    u = pltpu.bitcast(x, jnp.uint32)
    r = (u + jnp.uint32(0x7FFF) + ((u >> 16) & jnp.uint32(1))) >> 16
    return (r[:, n:] << 16) | r[:, :n]


def _unpack_bf16_pairs(w):
    lo = pltpu.bitcast(w << 16, F32).astype(BF16)
    hi = pltpu.bitcast(w & jnp.uint32(0xFFFF0000), F32).astype(BF16)
    return lo, hi


def _mm_kernel(a_ref, b_ref, o_ref):
    o_ref[...] = jnp.dot(a_ref[...].astype(BF16), b_ref[...].astype(BF16),
                         preferred_element_type=F32).astype(o_ref.dtype)


def _mm_cast_kernel(a_ref, b_ref, o_ref, abf_ref):
    @pl.when(pl.program_id(1) == 0)
    def _():
        abf_ref[...] = a_ref[...].astype(BF16)

    o_ref[...] = jnp.dot(abf_ref[...], b_ref[...], preferred_element_type=F32).astype(o_ref.dtype)


def _mm_res_kernel(a_ref, b_ref, r_ref, o_ref, *, alpha):
    acc = jnp.dot(a_ref[...].astype(BF16), b_ref[...].astype(BF16), preferred_element_type=F32)
    o_ref[...] = (alpha * r_ref[...] + acc).astype(o_ref.dtype)


def matmul(a, b, *, tm, tn, out_dtype, res=None, alpha=1.0, name="matmul"):
    M, K = a.shape
    N = b.shape[1]
    assert M % tm == 0 and N % tn == 0
    in_specs = [pl.BlockSpec((tm, K), lambda i, j: (i, 0)),
                pl.BlockSpec((K, tn), lambda i, j: (0, j))]
    args = [a, b]
    scratch = []
    if res is None and a.dtype == F32:
        body = _mm_cast_kernel
        scratch = [pltpu.VMEM((tm, K), BF16)]
    elif res is None:
        body = _mm_kernel
    else:
        body = functools.partial(_mm_res_kernel, alpha=alpha)
        in_specs.append(pl.BlockSpec((tm, tn), lambda i, j: (i, j)))
        args.append(res)
    return pl.pallas_call(
        body,
        grid=(M // tm, N // tn),
        in_specs=in_specs,
        out_specs=pl.BlockSpec((tm, tn), lambda i, j: (i, j)),
        out_shape=jax.ShapeDtypeStruct((M, N), out_dtype),
        scratch_shapes=scratch,
        compiler_params=_cparams(2),
        name=name,
    )(*args)


def _rope_kernel(pos_ref, freq_ref, h_ref, o_ref, *, n_q_heads, scale):
    ang = pos_ref[...] * freq_ref[...]
    lane = lax.broadcasted_iota(jnp.int32, ang.shape, 1)
    cos = jnp.cos(ang)
    sin = jnp.sin(ang)
    half = ROPE_DIM // 2
    c_mul = jnp.where(lane < ROPE_DIM, cos, 1.0)
    s_lo = jnp.where(lane < half, -sin, 0.0)
    s_hi = jnp.where((lane >= half) & (lane < ROPE_DIM), sin, 0.0)
    n_heads = h_ref.shape[1] // HEAD_DIM
    for hd in range(n_heads):
        t = h_ref[:, hd * HEAD_DIM:(hd + 1) * HEAD_DIM].astype(F32)
        up = pltpu.roll(t, HEAD_DIM - half, axis=1)
        dn = pltpu.roll(t, half, axis=1)
        r = t * c_mul + up * s_lo + dn * s_hi
        if hd < n_q_heads:
            r = r * scale
        o_ref[:, hd * HEAD_DIM:(hd + 1) * HEAD_DIM] = r.astype(o_ref.dtype)


def rope(h, pos_col, freq_row, *, n_q_heads, n_k_heads, scale, tm):
    T = h.shape[0]
    width = (n_q_heads + n_k_heads) * HEAD_DIM
    return pl.pallas_call(
        functools.partial(_rope_kernel, n_q_heads=n_q_heads, scale=scale),
        grid=(T // tm,),
        in_specs=[pl.BlockSpec((tm, 1), lambda i: (i, 0)),
                  pl.BlockSpec((1, HEAD_DIM), lambda i: (0, 0)),
                  pl.BlockSpec((tm, width), lambda i: (i, 0))],
        out_specs=pl.BlockSpec((tm, width), lambda i: (i, 0)),
        out_shape=jax.ShapeDtypeStruct((T, width), BF16),
        compiler_params=_cparams(1),
        name="rope",
    )(pos_col, freq_row, h)


def _swa_kernel(bias_ref, sink_ref, q_ref, kc_ref, kp_ref, vc_ref, vp_ref, o_ref):
    bias = bias_ref[0]
    qw = KV_GROUP * HEAD_DIM
    n_kv_heads = kc_ref.shape[1] // HEAD_DIM
    for g in range(n_kv_heads):
        kv_cols = slice(g * HEAD_DIM, (g + 1) * HEAD_DIM)
        q4 = jnp.concatenate(
            [q_ref[:, g * qw + i * HEAD_DIM:g * qw + (i + 1) * HEAD_DIM] for i in range(KV_GROUP)], axis=0)
        k = jnp.concatenate([kp_ref[:, kv_cols], kc_ref[:, kv_cols]], axis=0)
        v = jnp.concatenate([vp_ref[:, kv_cols], vc_ref[:, kv_cols]], axis=0)
        s = lax.dot_general(q4, k, (((1,), (1,)), ((), ())), preferred_element_type=F32) + bias
        sink = sink_ref[g][:, :1]
        m = jnp.maximum(jnp.max(s, axis=-1, keepdims=True), sink)
        p = jnp.exp(s - m)
        denom = jnp.sum(p, axis=-1, keepdims=True) + jnp.exp(sink - m)
        o = jnp.dot(p.astype(BF16), v, preferred_element_type=F32) / denom
        for i in range(KV_GROUP):
            o_ref[:, g * qw + i * HEAD_DIM:g * qw + (i + 1) * HEAD_DIM] = (
                o[i * ATTN_BLOCK:(i + 1) * ATTN_BLOCK].astype(o_ref.dtype))


def sliding_window_attention(qk, h, bias, sink_rows, *, batch, seq, n_kv_heads, k_col0, v_col0):
    nb = seq // ATTN_BLOCK
    T = batch * seq
    q_width = n_kv_heads * KV_GROUP * HEAD_DIM
    kv_width = n_kv_heads * HEAD_DIM

    def cur(b, n):
        return b * nb + n

    def prev(b, n):
        return b * nb + jnp.maximum(n - 1, 0)

    return pl.pallas_call(
        _swa_kernel,
        grid=(batch, nb),
        in_specs=[
            pl.BlockSpec((1, KV_GROUP * ATTN_BLOCK, 2 * ATTN_BLOCK), lambda b, n: (jnp.minimum(n, 1), 0, 0)),
            pl.BlockSpec((n_kv_heads, KV_GROUP * ATTN_BLOCK, V7X_LANES), lambda b, n: (0, 0, 0)),
            pl.BlockSpec((ATTN_BLOCK, q_width), lambda b, n: (cur(b, n), 0)),
            pl.BlockSpec((ATTN_BLOCK, kv_width), lambda b, n: (cur(b, n), k_col0)),
            pl.BlockSpec((ATTN_BLOCK, kv_width), lambda b, n: (prev(b, n), k_col0)),
            pl.BlockSpec((ATTN_BLOCK, kv_width), lambda b, n: (cur(b, n), v_col0)),
            pl.BlockSpec((ATTN_BLOCK, kv_width), lambda b, n: (prev(b, n), v_col0)),
        ],
        out_specs=pl.BlockSpec((ATTN_BLOCK, q_width), lambda b, n: (cur(b, n), 0)),
        out_shape=jax.ShapeDtypeStruct((T, q_width), BF16),
        compiler_params=_cparams(2),
        name="swa",
    )(bias, sink_rows, qk, qk, qk, h, h)


def _swa_bias():
    qi = np.arange(ATTN_BLOCK)[:, None]
    si = np.arange(2 * ATTN_BLOCK)[None, :]
    delta = ATTN_BLOCK + qi - si
    band = (delta >= 0) & (delta < ATTN_BLOCK)
    first = band & (si >= ATTN_BLOCK)
    m = np.stack([first, band]).astype(np.float32)
    bias = np.where(m > 0, 0.0, -np.inf).astype(np.float32)
    return np.tile(bias, (1, KV_GROUP, 1))


def _gelu_tanh(y):
    return 0.5 * y * (1.0 + jnp.tanh(0.7978845608028654 * (y + 0.044715 * (y * y * y))))


def _lru_kernel(x_ref, y_ref, cw_ref, cb_ref, wa_ref, ba_ref, wi_ref, bi_ref, lam_ref, o_ref,
                tail_ref, carry_ref):
    tt = x_ref.shape[0]

    @pl.when(pl.program_id(2) == 0)
    def _():
        tail_ref[...] = jnp.zeros_like(tail_ref)
        carry_ref[...] = jnp.zeros_like(carry_ref)

    x = x_ref[...].astype(F32)
    ext = jnp.concatenate([tail_ref[...], x], axis=0)
    tail_ref[...] = x[tt - V7X_SUBLANES:, :]
    cw = cw_ref[...]
    xc = cb_ref[...] + cw[CONV_WIDTH - 1:CONV_WIDTH, :] * x
    for j in range(1, CONV_WIDTH):
        sh = pltpu.roll(ext, j, axis=0)[V7X_SUBLANES:, :]
        xc = xc + cw[CONV_WIDTH - 1 - j:CONV_WIDTH - j, :] * sh
    xcb = xc.astype(BF16)
    r = jax.nn.sigmoid(jnp.dot(xcb, wa_ref[0].astype(BF16), preferred_element_type=F32) + ba_ref[0])
    gi = jax.nn.sigmoid(jnp.dot(xcb, wi_ref[0].astype(BF16), preferred_element_type=F32) + bi_ref[0])
    z = -lam_ref[...]
    softplus = jnp.maximum(z, 0.0) + jnp.log(1.0 + jnp.exp(-jnp.abs(z)))
    log_a = (-LRU_C) * r * softplus
    a = jnp.exp(log_a)
    b = jnp.sqrt(1.0 - a * a) * (gi * xc)

    row = lax.broadcasted_iota(jnp.int32, a.shape, 0)
    d = 1
    while d < tt:
        if d < V7X_SUBLANES:
            a_sh = jnp.where(row >= d, pltpu.roll(a, d, axis=0), 1.0)
            b_sh = jnp.where(row >= d, pltpu.roll(b, d, axis=0), 0.0)
            b = a * b_sh + b
            a = a * a_sh
        else:
            b = jnp.concatenate([b[:d], a[d:] * b[:-d] + b[d:]], axis=0)
            a = jnp.concatenate([a[:d], a[d:] * a[:-d]], axis=0)
        d *= 2
    h_prev = carry_ref[V7X_SUBLANES - 1:V7X_SUBLANES, :]
    h = a * h_prev + b
    carry_ref[...] = h[tt - V7X_SUBLANES:, :]
    o_ref[...] = (h * _gelu_tanh(y_ref[...].astype(F32))).astype(o_ref.dtype)


def rglru(h, conv_w, conv_b, w_a, b_a, w_i, b_i, lam, *, batch, seq, x_col0, y_col0, tt):
    nblk = w_a.shape[0]
    C = nblk * LRU_BLOCK_DIM
    nt = seq // tt
    T = batch * seq
    cb = LRU_BLOCK_DIM
    vec = lambda b, c, t: (0, c)
    blk3 = lambda b, c, t: (c, 0, 0)
    return pl.pallas_call(
        _lru_kernel,
        grid=(batch, nblk, nt),
        in_specs=[
            pl.BlockSpec((tt, cb), lambda b, c, t: (b * nt + t, x_col0 + c)),
            pl.BlockSpec((tt, cb), lambda b, c, t: (b * nt + t, y_col0 + c)),
            pl.BlockSpec((CONV_WIDTH, cb), vec),
            pl.BlockSpec((1, cb), vec),
            pl.BlockSpec((1, cb, cb), blk3),
            pl.BlockSpec((1, 1, cb), blk3),
            pl.BlockSpec((1, cb, cb), blk3),
            pl.BlockSpec((1, 1, cb), blk3),
            pl.BlockSpec((1, cb), vec),
        ],
        out_specs=pl.BlockSpec((tt, cb), lambda b, c, t: (b * nt + t, c)),
        out_shape=jax.ShapeDtypeStruct((T, C), BF16),
        scratch_shapes=[pltpu.VMEM((V7X_SUBLANES, cb), F32), pltpu.VMEM((V7X_SUBLANES, cb), F32)],
        compiler_params=_cparams(3),
        name="rglru",
    )(h, h, conv_w, conv_b.reshape(1, C), w_a, b_a.reshape(nblk, 1, cb), w_i,
      b_i.reshape(nblk, 1, cb), lam.reshape(1, C))


def _merge_kernel(oa_ref, ol_ref, wa_ref, wl_ref, ga_ref, gl_ref, bg_ref, o_ref):
    pa = jnp.dot(oa_ref[...], wa_ref[...].astype(BF16), preferred_element_type=F32)
    pb = jnp.dot(ol_ref[...], wl_ref[...].astype(BF16), preferred_element_type=F32)
    bg = bg_ref[...]
    g_a = jax.nn.sigmoid(ga_ref[...].astype(F32) + bg[0:1, :])
    g_l = jax.nn.sigmoid(gl_ref[...].astype(F32) + bg[1:2, :])
    o_ref[...] = (g_a * pa + g_l * pb).astype(o_ref.dtype)


def gated_merge(oa, ol, wa, wl, h, b_gate, *, ga_col0, gl_col0, tm, tn):
    T, K = oa.shape
    N = wa.shape[1]
    return pl.pallas_call(
        _merge_kernel,
        grid=(T // tm, N // tn),
        in_specs=[
            pl.BlockSpec((tm, K), lambda i, j: (i, 0)),
            pl.BlockSpec((tm, K), lambda i, j: (i, 0)),
            pl.BlockSpec((K, tn), lambda i, j: (0, j)),
            pl.BlockSpec((K, tn), lambda i, j: (0, j)),
            pl.BlockSpec((tm, tn), lambda i, j: (i, ga_col0 + j)),
            pl.BlockSpec((tm, tn), lambda i, j: (i, gl_col0 + j)),
            pl.BlockSpec((2, tn), lambda i, j: (0, j)),
        ],
        out_specs=pl.BlockSpec((tm, tn), lambda i, j: (i, j)),
        out_shape=jax.ShapeDtypeStruct((T, N), BF16),
        compiler_params=_cparams(2),
        name="gated_merge",
    )(oa, ol, wa, wl, h, h, b_gate)


def _xattn_kernel(y_ref, g_ref, b_ref, wq_ref, kv_ref, wo_ref, o_ref, *, scale, alpha):
    x1 = _layer_norm_rows(y_ref[...], g_ref[...], b_ref[...])
    q = jnp.dot(x1.astype(BF16), wq_ref[...], preferred_element_type=F32) * scale
    qb = q.astype(BF16)
    width = MEM_HEADS * MEM_HEAD_DIM
    outs = []
    for hd in range(MEM_HEADS):
        sl = slice(hd * MEM_HEAD_DIM, (hd + 1) * MEM_HEAD_DIM)
        k = kv_ref[:, sl]
        v = kv_ref[:, width + hd * MEM_HEAD_DIM:width + (hd + 1) * MEM_HEAD_DIM]
        s = lax.dot_general(qb[:, sl], k, (((1,), (1,)), ((), ())), preferred_element_type=F32)
        m = jnp.max(s, axis=-1, keepdims=True)
        p = jnp.exp(s - m)
        den = jnp.sum(p, axis=-1, keepdims=True)
        outs.append(jnp.dot(p.astype(BF16), v, preferred_element_type=F32) / den)
    o = jnp.concatenate(outs, axis=1).astype(BF16)
    xa = jnp.dot(o, wo_ref[...], preferred_element_type=F32)
    o_ref[...] = alpha * x1 + xa


def memory_cross_attention(y1, ln_g, ln_b, wq, kv, wo, *, seq, mem_len, tm):
    T, D = y1.shape
    per_batch = seq // tm
    width = MEM_HEADS * MEM_HEAD_DIM
    return pl.pallas_call(
        functools.partial(_xattn_kernel, scale=MEM_HEAD_DIM ** -0.5, alpha=DEEPNORM_ALPHA),
        grid=(T // tm,),
        in_specs=[
            pl.BlockSpec((tm, D), lambda i: (i, 0)),
            pl.BlockSpec((1, D), lambda i: (0, 0)),
            pl.BlockSpec((1, D), lambda i: (0, 0)),
            pl.BlockSpec((D, width), lambda i: (0, 0)),
            pl.BlockSpec((mem_len, 2 * width), lambda i: (i // per_batch, 0)),
            pl.BlockSpec((width, D), lambda i: (0, 0)),
        ],
        out_specs=pl.BlockSpec((tm, D), lambda i: (i, 0)),
        out_shape=jax.ShapeDtypeStruct((T, D), F32),
        compiler_params=_cparams(1),
        name="mem_xattn",
    )(y1, ln_g.reshape(1, D), ln_b.reshape(1, D), wq, kv, wo)


def _router_kernel(y_ref, g_ref, b_ref, wr_ref, br_ref, x2_ref, idx_ref, rank_ref, wts_ref, cnt_ref,
                   carry_ref):
    tm = y_ref.shape[0]

    @pl.when(pl.program_id(0) == 0)
    def _():
        carry_ref[...] = jnp.zeros_like(carry_ref)

    x2 = _layer_norm_rows(y_ref[...], g_ref[...], b_ref[...])
    x2_ref[...] = _pack_bf16_pairs(x2)
    logits = lax.dot_general(wr_ref[...], x2, (((1,), (1,)), ((), ())),
                             precision=lax.Precision.HIGHEST, preferred_element_type=F32)
    logits = logits + br_ref[...][:, :1]
    eidx = lax.broadcasted_iota(jnp.int32, logits.shape, 0)
    work = logits
    vals, idxs, hots = [], [], []
    for _ in range(TOP_K):
        m = jnp.max(work, axis=0, keepdims=True)
        idx = jnp.min(jnp.where(work == m, eidx, N_EXPERTS), axis=0, keepdims=True)
        hot = eidx == idx
        vals.append(m)
        idxs.append(idx)
        hots.append(hot)
        work = jnp.where(hot, -jnp.inf, work)
    exps = [jnp.exp(v - vals[0]) for v in vals]
    den = exps[0] + exps[1] + exps[2] + exps[3]
    zero_row = jnp.zeros_like(den)
    wts_ref[...] = jnp.concatenate([e / den for e in exps] + [zero_row] * (V7X_SUBLANES - TOP_K), axis=0)

    sel = jnp.zeros(logits.shape, F32)
    for hot in hots:
        sel = sel + jnp.where(hot, 1.0, 0.0)
    r_i = lax.broadcasted_iota(jnp.int32, (tm, tm), 0)
    c_i = lax.broadcasted_iota(jnp.int32, (tm, tm), 1)
    upper = jnp.where(r_i <= c_i, 1.0, 0.0).astype(BF16)
    incl = jnp.dot(sel.astype(BF16), upper, preferred_element_type=F32)
    carry = carry_ref[...][:, :1]
    excl = carry + incl - sel
    ranks = [jnp.sum(jnp.where(hot, excl, 0.0), axis=0, keepdims=True) for hot in hots]
    rank_ref[...] = jnp.concatenate(ranks, axis=0).astype(jnp.int32)
    idx_ref[...] = jnp.concatenate(idxs, axis=0)
    new_carry = carry + incl[:, tm - 1:tm]
    carry_ref[...] = jnp.broadcast_to(new_carry, carry_ref.shape)
    cnt_ref[...] = jnp.broadcast_to(new_carry, cnt_ref.shape)


def router(y2, ln_g, ln_b, w_router_t, b_router, *, tm):
    T, D = y2.shape
    E = w_router_t.shape[0]
    br = jnp.broadcast_to(b_router.reshape(E, 1), (E, V7X_LANES))
    return pl.pallas_call(
        _router_kernel,
        grid=(T // tm,),
        in_specs=[
            pl.BlockSpec((tm, D), lambda i: (i, 0)),
            pl.BlockSpec((1, D), lambda i: (0, 0)),
            pl.BlockSpec((1, D), lambda i: (0, 0)),
            pl.BlockSpec((E, D), lambda i: (0, 0)),
            pl.BlockSpec((E, V7X_LANES), lambda i: (0, 0)),
        ],
        out_specs=[
            pl.BlockSpec((tm, D // 2), lambda i: (i, 0)),
            pl.BlockSpec((TOP_K, tm), lambda i: (0, i)),
            pl.BlockSpec((TOP_K, tm), lambda i: (0, i)),
            pl.BlockSpec((V7X_SUBLANES, tm), lambda i: (0, i)),
            pl.BlockSpec((E, V7X_LANES), lambda i: (0, 0)),
        ],
        out_shape=[
            jax.ShapeDtypeStruct((T, D // 2), jnp.uint32),
            jax.ShapeDtypeStruct((TOP_K, T), jnp.int32),
            jax.ShapeDtypeStruct((TOP_K, T), jnp.int32),
            jax.ShapeDtypeStruct((V7X_SUBLANES, T), F32),
            jax.ShapeDtypeStruct((E, V7X_LANES), F32),
        ],
        scratch_shapes=[pltpu.VMEM((E, V7X_LANES), F32)],
        compiler_params=_cparams(1),
        name="router",
    )(y2, ln_g.reshape(1, D), ln_b.reshape(1, D), w_router_t, br)


def _pos_kernel(idx_ref, rank_ref, off_ref, pos_ref):
    off = off_ref[...][:, :1]
    shape = (off_ref.shape[0], idx_ref.shape[1])
    eidx = lax.broadcasted_iota(jnp.int32, shape, 0)
    rows = []
    for k in range(TOP_K):
        hot = eidx == idx_ref[k:k + 1, :]
        rows.append(jnp.sum(jnp.where(hot, off, 0.0), axis=0, keepdims=True))
    pos_ref[...] = jnp.concatenate(rows, axis=0).astype(jnp.int32) + rank_ref[...]


def grouped_positions(idx, rank, offsets_f, *, tm):
    T = idx.shape[1]
    E = offsets_f.shape[0]
    return pl.pallas_call(
        _pos_kernel,
        grid=(T // tm,),
        in_specs=[pl.BlockSpec((TOP_K, tm), lambda i: (0, i)),
                  pl.BlockSpec((TOP_K, tm), lambda i: (0, i)),
                  pl.BlockSpec((E, V7X_LANES), lambda i: (0, 0))],
        out_specs=pl.BlockSpec((TOP_K, tm), lambda i: (0, i)),
        out_shape=jax.ShapeDtypeStruct((TOP_K, T), jnp.int32),
        compiler_params=_cparams(1),
        name="grouped_pos",
    )(idx, rank, offsets_f)


def _dispatch_kernel(pos_ref, pad_ref, x_ref, xg_ref, zero_ref, sem, zero_sem):
    td = x_ref.shape[0]
    tm = zero_ref.shape[0]

    @pl.when(pl.program_id(0) == 0)
    def _():
        zero_ref[...] = jnp.zeros_like(zero_ref)

        def zero_copy(w):
            return pltpu.make_async_copy(zero_ref, xg_ref.at[pl.ds(pl.multiple_of(w * tm, tm), tm)], zero_sem)

        def start(w, c):
            @pl.when(pad_ref[w] > 0)
            def _():
                zero_copy(w).start()
            return c

        def wait(w, c):
            @pl.when(pad_ref[w] > 0)
            def _():
                zero_copy(w).wait()
            return c

        lax.fori_loop(0, pad_ref.shape[0], start, 0)
        lax.fori_loop(0, pad_ref.shape[0], wait, 0)

    def row_copy(t, p):
        return pltpu.make_async_copy(x_ref.at[pl.ds(t, 1)], xg_ref.at[pl.ds(p, 1)], sem)

    def issue(t, c):
        for k in range(TOP_K):
            row_copy(t, pos_ref[k, t]).start()
        return c

    lax.fori_loop(0, td, issue, 0)

    for k in range(TOP_K):
        pltpu.make_async_copy(x_ref, xg_ref.at[pl.ds(0, td)], sem).wait()


def dispatch(x2, pos, pad_tiles, *, tm, td):
    T, D = x2.shape
    rows = pad_tiles.shape[0] * tm
    return pl.pallas_call(
        _dispatch_kernel,
        grid=(T // td,),
        in_specs=[pl.BlockSpec((TOP_K, td), lambda i: (0, i), memory_space=pltpu.SMEM),
                  pl.BlockSpec(memory_space=pltpu.SMEM),
                  pl.BlockSpec((td, D), lambda i: (i, 0))],
        out_specs=pl.BlockSpec(memory_space=pl.ANY),
        out_shape=jax.ShapeDtypeStruct((rows, D), x2.dtype),
        scratch_shapes=[pltpu.VMEM((tm, D), x2.dtype), pltpu.SemaphoreType.DMA(()), pltpu.SemaphoreType.DMA(())],
        compiler_params=_cparams(1),
        name="moe_dispatch",
    )(pos, pad_tiles, x2)


def _expert_weight_stream(w_hbm, wbuf, sem, maps, *, width, hi_col0):
    te_ref, tv_ref, _, first_ref, next_ref, slot_ref = maps
    j = pl.program_id(0)
    w = pl.program_id(1)
    slot = slot_ref[w]
    valid = tv_ref[w] > 0

    def slab_copies(e, s):
        return [pltpu.make_async_copy(
            w_hbm.at[e, :, pl.ds(pl.multiple_of(j * width + c * hi_col0, V7X_LANES), width)],
            wbuf.at[s, c], sem.at[s]) for c in range(2)]

    @pl.when(w == 0)
    def _():
        for cp in slab_copies(te_ref[0], slot):
            cp.start()

    @pl.when(valid & (first_ref[w] > 0))
    def _():
        for cp in slab_copies(te_ref[w], slot):
            cp.wait()

        @pl.when(next_ref[w] >= 0)
        def _():
            for cp in slab_copies(next_ref[w], 1 - slot):
                cp.start()

    return valid, slot


def _moe_up_kernel(te_ref, tv_ref, tr_ref, first_ref, next_ref, slot_ref, x_ref, w_hbm, bg_ref, bu_ref, o_ref,
                   wbuf, sem):
    maps = (te_ref, tv_ref, tr_ref, first_ref, next_ref, slot_ref)
    tf = o_ref.shape[1]
    valid, slot = _expert_weight_stream(w_hbm, wbuf, sem, maps, width=tf, hi_col0=w_hbm.shape[2] // 2)

    @pl.when(valid)
    def _():
        x_lo, x_hi = _unpack_bf16_pairs(x_ref[...])
        n = x_lo.shape[1]

        def proj(c, b_ref):
            return (jnp.dot(x_lo, wbuf[slot, c, :n, :].astype(BF16), preferred_element_type=F32)
                    + jnp.dot(x_hi, wbuf[slot, c, n:, :].astype(BF16), preferred_element_type=F32) + b_ref[0])

        gate = proj(0, bg_ref)
        up = proj(1, bu_ref)
        gate = jnp.minimum(gate, SWIGLU_LIMIT)
        up = jnp.clip(up, -SWIGLU_LIMIT, SWIGLU_LIMIT)
        glu = gate * jax.nn.sigmoid(SWIGLU_ALPHA * gate)
        o_ref[...] = ((up + 1.0) * glu).astype(o_ref.dtype)

    @pl.when(jnp.logical_not(valid))
    def _():
        o_ref[...] = jnp.zeros_like(o_ref)


def moe_up(xg, w_gate_up, b_gate_up, tile_maps, *, tm, tf):
    R, half_d = xg.shape
    E, D, F2 = w_gate_up.shape
    assert D == 2 * half_d
    F = F2 // 2
    nf = F // tf
    nt = R // tm
    grid_spec = pltpu.PrefetchScalarGridSpec(
        num_scalar_prefetch=len(tile_maps),
        grid=(nf, nt),
        in_specs=[
            pl.BlockSpec((tm, half_d), lambda j, w, te, tv, tr, *_: (tr[w], 0)),
            pl.BlockSpec(memory_space=pl.ANY),
            pl.BlockSpec((1, 1, tf), lambda j, w, te, tv, tr, *_: (te[w], 0, j)),
            pl.BlockSpec((1, 1, tf), lambda j, w, te, tv, tr, *_: (te[w], 0, nf + j)),
        ],
        out_specs=pl.BlockSpec((tm, tf), lambda j, w, *_: (w, j)),
        scratch_shapes=[pltpu.VMEM((2, 2, D, tf), w_gate_up.dtype), pltpu.SemaphoreType.DMA((2,))],
    )
    b3 = b_gate_up.reshape(E, 1, F2)
    return pl.pallas_call(
        _moe_up_kernel,
        grid_spec=grid_spec,
        out_shape=jax.ShapeDtypeStruct((R, F), BF16),
        compiler_params=_cparams(2),
        name="moe_up",
    )(*tile_maps, xg, w_gate_up, b3, b3)


def _moe_down_kernel(te_ref, tv_ref, tr_ref, first_ref, next_ref, slot_ref, h_ref, w_hbm, blo_ref, bhi_ref, o_ref,
                     wbuf, sem):
    maps = (te_ref, tv_ref, tr_ref, first_ref, next_ref, slot_ref)
    tn = o_ref.shape[1]
    valid, slot = _expert_weight_stream(w_hbm, wbuf, sem, maps, width=tn, hi_col0=w_hbm.shape[2] // 2)

    @pl.when(valid)
    def _():
        h = h_ref[...]
        lo = jnp.dot(h, wbuf[slot, 0].astype(BF16), preferred_element_type=F32) + blo_ref[0]
        hi = jnp.dot(h, wbuf[slot, 1].astype(BF16), preferred_element_type=F32) + bhi_ref[0]
        o_ref[...] = _pack_bf16_pairs(jnp.concatenate([lo, hi], axis=1))

    @pl.when(jnp.logical_not(valid))
    def _():
        o_ref[...] = jnp.zeros_like(o_ref)


def moe_down(hg, w_down, b_down, tile_maps, *, tm, tn):
    R, F = hg.shape
    E, _, D = w_down.shape
    half = D // 2
    nn = half // tn
    nt = R // tm
    grid_spec = pltpu.PrefetchScalarGridSpec(
        num_scalar_prefetch=len(tile_maps),
        grid=(nn, nt),
        in_specs=[
            pl.BlockSpec((tm, F), lambda j, w, te, tv, tr, *_: (tr[w], 0)),
            pl.BlockSpec(memory_space=pl.ANY),
            pl.BlockSpec((1, 1, tn), lambda j, w, te, tv, tr, *_: (te[w], 0, j)),
            pl.BlockSpec((1, 1, tn), lambda j, w, te, tv, tr, *_: (te[w], 0, nn + j)),
        ],
        out_specs=pl.BlockSpec((tm, tn), lambda j, w, *_: (w, j)),
        scratch_shapes=[pltpu.VMEM((2, 2, F, tn), w_down.dtype), pltpu.SemaphoreType.DMA((2,))],
    )
    b3 = b_down.reshape(E, 1, D)
    return pl.pallas_call(
        _moe_down_kernel,
        grid_spec=grid_spec,
        out_shape=jax.ShapeDtypeStruct((R, half), jnp.uint32),
        compiler_params=_cparams(2),
        name="moe_down",
    )(*tile_maps, hg, w_down, b3, b3)


COMBINE_ROW_CHUNK = 64


def _combine_kernel(pos_ref, pos_next_ref, y2_ref, w_ref, g2_ref, b2_ref, g_ref, b_ref, yg_ref, o_ref,
                    buf_ref, wcol_ref, sem, *, alpha):
    tc = y2_ref.shape[0]
    i = pl.program_id(0)
    slot = lax.rem(i, 2)

    def row_copy(s, k, t, p):
        return pltpu.make_async_copy(yg_ref.at[pl.ds(p, 1)], buf_ref.at[s, k, pl.ds(t, 1)], sem.at[s])

    def gather(p_ref, s):
        def issue(t, c):
            for k in range(TOP_K):
                row_copy(s, k, t, p_ref[k, t]).start()
            return c
        lax.fori_loop(0, tc, issue, 0)

    @pl.when(i == 0)
    def _():
        gather(pos_ref, 0)

    @pl.when(i + 1 < pl.num_programs(0))
    def _():
        gather(pos_next_ref, 1 - slot)

    for k in range(TOP_K):
        pltpu.make_async_copy(yg_ref.at[pl.ds(0, tc)], buf_ref.at[slot, k], sem.at[slot]).wait()

    wt = w_ref[...]
    wt = jnp.concatenate([wt, jnp.zeros((V7X_LANES - wt.shape[0], tc), F32)], axis=0)
    wcol_ref[...] = wt.T

    def chunk(c, carry):
        rows = pl.ds(pl.multiple_of(c * COMBINE_ROW_CHUNK, COMBINE_ROW_CHUNK), COMBINE_ROW_CHUNK)
        y = alpha * _layer_norm_rows(y2_ref[rows, :], g2_ref[...], b2_ref[...])
        w_cols = wcol_ref[rows, :]
        for k in range(TOP_K):
            word = buf_ref[slot, k, rows, :]
            lo = pltpu.bitcast(word << 16, F32)
            hi = pltpu.bitcast(word & jnp.uint32(0xFFFF0000), F32)
            y = y + w_cols[:, k:k + 1] * jnp.concatenate([lo, hi], axis=1)
        o_ref[rows, :] = _layer_norm_rows(y, g_ref[...], b_ref[...])
        return carry

    lax.fori_loop(0, tc // COMBINE_ROW_CHUNK, chunk, 0)


def combine(y2, wts, pos, yg, ln2_g, ln2_b, ln_g, ln_b, *, tc):
    T, D = y2.shape
    assert tc == V7X_LANES and yg.shape[1] * 2 == D
    n_steps = T // tc
    return pl.pallas_call(
        functools.partial(_combine_kernel, alpha=DEEPNORM_ALPHA),
        grid=(n_steps,),
        in_specs=[
            pl.BlockSpec((TOP_K, tc), lambda i: (0, i), memory_space=pltpu.SMEM),
            pl.BlockSpec((TOP_K, tc), lambda i: (0, jnp.minimum(i + 1, n_steps - 1)), memory_space=pltpu.SMEM),
            pl.BlockSpec((tc, D), lambda i: (i, 0)),
            pl.BlockSpec((V7X_SUBLANES, tc), lambda i: (0, i)),
            pl.BlockSpec((1, D), lambda i: (0, 0)),
            pl.BlockSpec((1, D), lambda i: (0, 0)),
            pl.BlockSpec((1, D), lambda i: (0, 0)),
            pl.BlockSpec((1, D), lambda i: (0, 0)),
            pl.BlockSpec(memory_space=pl.ANY),
        ],
        out_specs=pl.BlockSpec((tc, D), lambda i: (i, 0)),
        out_shape=jax.ShapeDtypeStruct((T, D), F32),
        scratch_shapes=[pltpu.VMEM((2, TOP_K, tc, D // 2), jnp.uint32),
                        pltpu.VMEM((tc, V7X_LANES), F32),
                        pltpu.SemaphoreType.DMA((2,))],
        compiler_params=_cparams(1),
        name="moe_combine",
    )(pos, pos, y2, wts, ln2_g.reshape(1, D), ln2_b.reshape(1, D), ln_g.reshape(1, D), ln_b.reshape(1, D), yg)


def _tile_maps(counts, *, tm, n_tiles):
    counts = counts.astype(jnp.int32)
    tiles_per = (counts + tm - 1) // tm
    tile_end = jnp.cumsum(tiles_per)
    offsets = (tile_end - tiles_per) * tm
    w = jnp.arange(n_tiles, dtype=jnp.int32)
    te = jnp.sum((w[:, None] >= tile_end[None, :]).astype(jnp.int32), axis=1)
    valid = (w < tile_end[-1]).astype(jnp.int32)
    last_e = jnp.max(jnp.where(tiles_per > 0, jnp.arange(counts.shape[0], dtype=jnp.int32), 0))
    te = jnp.where(valid > 0, te, last_e).astype(jnp.int32)
    tr = jnp.minimum(w, tile_end[-1] - 1).astype(jnp.int32)
    prev_e = jnp.concatenate([jnp.full((1,), -1, jnp.int32), te[:-1]])
    first = ((valid > 0) & (te != prev_e)).astype(jnp.int32)
    slot = ((jnp.cumsum(first) - 1) % 2).astype(jnp.int32)
    first_pos = jnp.where(first > 0, w, n_tiles)
    next_first = jnp.concatenate([lax.cummin(first_pos[::-1])[::-1][1:], jnp.full((1,), n_tiles, jnp.int32)])
    nxt = jnp.where(next_first < n_tiles, te[jnp.minimum(next_first, n_tiles - 1)], -1).astype(jnp.int32)
    next_is_first = jnp.concatenate([first[1:], jnp.ones((1,), jnp.int32)])
    pad = ((valid == 0) | (next_is_first > 0) | (w + 1 >= tile_end[-1])).astype(jnp.int32)
    return offsets, (te, valid, tr, first, nxt, slot), pad


def kernel(x, mem, positions, w_in, b_gate, attn_sinks, conv_w, conv_b, w_lru_a, b_lru_a, w_lru_i,
           b_lru_i, lru_lambda, w_branch_attn, w_branch_lru, w_mix_out, ln1_g, ln1_b, w_mem_q, w_mem_kv,
           w_mem_o, ln2_g, ln2_b, w_router, b_router, w_gate_up, b_gate_up, w_down, b_down, ln3_g, ln3_b):
    B, S, D = x.shape
    T = B * S
    l = 0
    n_q_heads = D // HEAD_DIM
    n_kv_heads = n_q_heads // KV_GROUP
    q_width = n_q_heads * HEAD_DIM
    kv_width = n_kv_heads * HEAD_DIM
    lru_width = w_lru_a.shape[1] * LRU_BLOCK_DIM
    col_xl = q_width + 2 * kv_width
    col_yl = col_xl + lru_width
    col_ga = col_yl + lru_width
    col_gl = col_ga + D

    xf = x.reshape(T, D)

    h = matmul(xf, w_in[l].astype(BF16), tm=1024, tn=512, out_dtype=BF16, name="in_proj")

    half = ROPE_DIM // 2
    inv_freq = 1.0 / (ROPE_THETA ** (np.arange(0, ROPE_DIM, 2, dtype=np.float32) / ROPE_DIM))
    freq_row = np.zeros((1, HEAD_DIM), np.float32)
    freq_row[0, :half] = inv_freq
    freq_row[0, half:ROPE_DIM] = inv_freq
    pos_col = positions.astype(F32).reshape(T, 1)
    qk = rope(h, pos_col, jnp.asarray(freq_row), n_q_heads=n_q_heads, n_k_heads=n_kv_heads,
              scale=HEAD_DIM ** -0.5, tm=256)
    sink_rows = jnp.broadcast_to(
        jnp.repeat(attn_sinks[l].astype(F32), ATTN_BLOCK).reshape(n_kv_heads, KV_GROUP * ATTN_BLOCK, 1),
        (n_kv_heads, KV_GROUP * ATTN_BLOCK, V7X_LANES))
    o_attn = sliding_window_attention(
        qk, h, jnp.asarray(_swa_bias()), sink_rows, batch=B, seq=S, n_kv_heads=n_kv_heads,
        k_col0=q_width // kv_width, v_col0=(q_width + kv_width) // kv_width)

    o_lru = rglru(h, conv_w[l], conv_b[l], w_lru_a[l], b_lru_a[l], w_lru_i[l], b_lru_i[l], lru_lambda[l],
                  batch=B, seq=S, x_col0=col_xl // LRU_BLOCK_DIM, y_col0=col_yl // LRU_BLOCK_DIM,
                  tt=LRU_TIME_TILE)

    tn_merge = 512
    mixed = gated_merge(o_attn, o_lru, w_branch_attn[l].astype(BF16), w_branch_lru[l].astype(BF16), h,
                        b_gate[l], ga_col0=col_ga // tn_merge, gl_col0=col_gl // tn_merge, tm=512, tn=tn_merge)
    y1 = matmul(mixed, w_mix_out[l].astype(BF16), tm=1024, tn=512, out_dtype=F32, res=xf,
                alpha=DEEPNORM_ALPHA, name="mix_out")

    M = mem.shape[1]
    kv = matmul(mem.reshape(B * M, D).astype(BF16), w_mem_kv[l].astype(BF16), tm=B * M, tn=512,
                out_dtype=BF16, name="mem_kv")
    y2 = memory_cross_attention(y1, ln1_g[l], ln1_b[l], w_mem_q[l].astype(BF16), kv,
                                w_mem_o[l].astype(BF16), seq=S, mem_len=M, tm=256)

    x2p, idx, rank, wts, cnt = router(y2, ln2_g[l], ln2_b[l], w_router[l].T, b_router[l], tm=256)
    n_tiles = (T * TOP_K) // MOE_ROW_TILE + N_EXPERTS
    offsets, tile_maps, pad_tiles = _tile_maps(cnt[:, 0], tm=MOE_ROW_TILE, n_tiles=n_tiles)
    offsets_f = jnp.broadcast_to(offsets.astype(F32)[:, None], (N_EXPERTS, V7X_LANES))
    pos = grouped_positions(idx, rank, offsets_f, tm=2048)

    xg = dispatch(x2p, pos, pad_tiles, tm=MOE_ROW_TILE, td=DISPATCH_TILE)
    hg = moe_up(xg, w_gate_up[l], b_gate_up[l], tile_maps, tm=MOE_ROW_TILE, tf=MOE_F_TILE)
    yg = moe_down(hg, w_down[l], b_down[l], tile_maps, tm=MOE_ROW_TILE, tn=MOE_N_TILE)
    out = combine(y2, wts, pos, yg, ln2_g[l], ln2_b[l], ln3_g[l], ln3_b[l], tc=COMBINE_TILE)
    return out.reshape(B, S, D)
```
